```python
import math
import jax, jax.numpy as jnp
from jax import lax
import numpy as np

D_MODEL = 1024
BATCH = 4
SEQ = 8192
DEPTH = 1
DEC_BATCH = 2
DEC_SEQ = 8192
PAST_LEN = 128

GRID_W = 64
HEAD_DIM = 64
HQ_A = 8
HKV_A = 2
GROUP_A = HQ_A // HKV_A
AX_DIM = HEAD_DIM // 2
AX_THETA = 10000.0
A_Q = HQ_A * HEAD_DIM
A_KV = HKV_A * HEAD_DIM
BR_A = A_Q
H_B = 4
B_QK = H_B * 2 * HEAD_DIM
B_V = H_B * 2 * HEAD_DIM
BR_B = B_V
ROT_DIM = HEAD_DIM // 4
ROPE_THETA = 500000.0
D_IN = A_Q + 2 * A_KV + BR_A + 2 * B_QK + B_V + BR_B + 2 * D_MODEL
Q_BLOCK = 128
EPS = 1e-6

kernel_name = "hybrid_gqa_axial_diffattn_encoder"


def rms_norm(x, g):
    x32 = x.astype(jnp.float32)
    y = x32 * lax.rsqrt(jnp.mean(x32 * x32, axis=-1, keepdims=True) + EPS)
    return (y * g.astype(jnp.float32)).astype(x.dtype)


def rope_tables(pos, dim, theta):
    inv = theta ** (-jnp.arange(0, dim, 2, dtype=jnp.float32) / dim)
    ang = pos.astype(jnp.float32)[:, None] * inv[None, :]
    ang = jnp.concatenate([ang, ang], axis=-1)
    return jnp.cos(ang), jnp.sin(ang)


def apply_rope(x, cos, sin):
    shape = (1, x.shape[1]) + (1,) * (x.ndim - 3) + (x.shape[-1],)
    cos = cos.reshape(shape).astype(x.dtype)
    sin = sin.reshape(shape).astype(x.dtype)
    half = x.shape[-1] // 2
    x1, x2 = x[..., :half], x[..., half:]
    rot = jnp.concatenate([-x2, x1], axis=-1)
    return x * cos + rot * sin


def gqa_attention(q, k, v):
    B, S = q.shape[0], q.shape[1]
    nb = S // Q_BLOCK
    scale = 1.0 / math.sqrt(HEAD_DIM)
    qb = q.reshape(B, nb, Q_BLOCK, HKV_A, GROUP_A, HEAD_DIM).transpose(1, 0, 2, 3, 4, 5)

    def one(qblk):
        s = jnp.einsum('bqgrd,bkgd->bgrqk', qblk, k).astype(jnp.float32) * scale
        p = jax.nn.softmax(s, axis=-1).astype(v.dtype)
        return jnp.einsum('bgrqk,bkgd->bqgrd', p, v)

    o = lax.map(one, qb)
    return o.transpose(1, 0, 2, 3, 4, 5).reshape(B, S, HQ_A * HEAD_DIM)


def diff_attention(q, k, v, lam):
    B, S = q.shape[0], q.shape[1]
    nb = S // Q_BLOCK
    scale = 1.0 / math.sqrt(HEAD_DIM)
    qb = q.reshape(B, nb, Q_BLOCK, H_B, 2, HEAD_DIM).transpose(1, 0, 2, 3, 4, 5)

    def one(qblk):
        s = jnp.einsum('bqhmd,bkhmd->bhmqk', qblk, k).astype(jnp.float32) * scale
        p = jax.nn.softmax(s, axis=-1)
        pd = (p[:, :, 0] - lam * p[:, :, 1]).astype(v.dtype)
        return jnp.einsum('bhqk,bkhe->bqhe', pd, v)

    o = lax.map(one, qb)
    return o.transpose(1, 0, 2, 3, 4).reshape(B, S, H_B, 2 * HEAD_DIM)


def encoder_layer(x, c, layer_idx, w_ada, b_ada, norm_g, w_in, qn_a, kn_a, qn_b, kn_b,
                  lam_q1, lam_k1, lam_q2, lam_k2, subln_g, w_proj_a, w_proj_b, w_out):
    B, S, _ = x.shape
    rows = S // GRID_W
    mod = jnp.einsum('bd,de->be', jax.nn.silu(c), w_ada) + b_ada
    shift, scale, gate = jnp.split(mod, 3, axis=-1)
    h = rms_norm(x, norm_g) * (1.0 + scale[:, None, :]) + shift[:, None, :]

    proj = jnp.einsum('bsd,de->bse', h, w_in)
    o1 = A_Q
    o2 = o1 + A_KV
    o3 = o2 + A_KV
    o4 = o3 + BR_A
    o5 = o4 + B_QK
    o6 = o5 + B_QK
    o7 = o6 + B_V
    o8 = o7 + BR_B
    o9 = o8 + D_MODEL
    qa, ka, va, za, qb, kb, vb, zb, ga, gb = jnp.split(proj, [o1, o2, o3, o4, o5, o6, o7, o8, o9], axis=-1)

    qa = rms_norm(qa.reshape(B, S, HQ_A, HEAD_DIM), qn_a)
    ka = rms_norm(ka.reshape(B, S, HKV_A, HEAD_DIM), kn_a)
    va = va.reshape(B, S, HKV_A, HEAD_DIM)
    row = jnp.repeat(jnp.arange(rows), GRID_W)
    col = jnp.tile(jnp.arange(GRID_W), rows)
    cos_r, sin_r = rope_tables(row, AX_DIM, AX_THETA)
    cos_c, sin_c = rope_tables(col, AX_DIM, AX_THETA)
    qa = jnp.concatenate([apply_rope(qa[..., :AX_DIM], cos_r, sin_r),
                          apply_rope(qa[..., AX_DIM:], cos_c, sin_c)], axis=-1)
    ka = jnp.concatenate([apply_rope(ka[..., :AX_DIM], cos_r, sin_r),
                          apply_rope(ka[..., AX_DIM:], cos_c, sin_c)], axis=-1)
    oa = gqa_attention(qa, ka, va) * jax.nn.silu(za)

    qb = rms_norm(qb.reshape(B, S, H_B, 2, HEAD_DIM), qn_b)
    kb = rms_norm(kb.reshape(B, S, H_B, 2, HEAD_DIM), kn_b)
    vb = vb.reshape(B, S, H_B, 2 * HEAD_DIM)
    cos_p, sin_p = rope_tables(jnp.arange(S), ROT_DIM, ROPE_THETA)
    qb = jnp.concatenate([apply_rope(qb[..., :ROT_DIM], cos_p, sin_p), qb[..., ROT_DIM:]], axis=-1)
    kb = jnp.concatenate([apply_rope(kb[..., :ROT_DIM], cos_p, sin_p), kb[..., ROT_DIM:]], axis=-1)
    lam_init = 0.8 - 0.6 * math.exp(-0.3 * layer_idx)
    lam = (jnp.exp(jnp.sum(lam_q1.astype(jnp.float32) * lam_k1.astype(jnp.float32)))
           - jnp.exp(jnp.sum(lam_q2.astype(jnp.float32) * lam_k2.astype(jnp.float32)))
           + lam_init)
    ob = diff_attention(qb, kb, vb, lam)
    ob = rms_norm(ob, subln_g) * (1.0 - lam_init)
    ob = ob.reshape(B, S, BR_B) * jax.nn.silu(zb)

    pa = jnp.einsum('bse,ed->bsd', oa, w_proj_a)
    pb = jnp.einsum('bse,ed->bsd', ob, w_proj_b)
    merged = jax.nn.sigmoid(ga) * pa + jax.nn.sigmoid(gb) * pb
    out = jnp.einsum('bsd,de->bse', merged, w_out)
    return x + gate[:, None, :] * out


def setup_inputs(seed: int = 0) -> dict:
    key = jax.random.key(seed)
    ks = jax.random.split(key, 20)
    f32 = jnp.float32
    D = D_MODEL

    def nrm(k, shape, s):
        return jax.random.normal(k, shape, f32) * s

    return {
        "x_prompt": nrm(ks[0], (BATCH, SEQ, D), 1.0),
        "x_sample": nrm(ks[1], (DEC_BATCH, DEC_SEQ, D), 1.0),
        "c_prompt": nrm(ks[2], (BATCH, D), 1.0),
        "c_sample": nrm(ks[3], (DEC_BATCH, D), 1.0),
        "w_ada": nrm(ks[4], (DEPTH, D, 3 * D), 0.5 * D ** -0.5),
        "b_ada": nrm(ks[5], (DEPTH, 3 * D), 0.01),
        "norm_g": 1.0 + nrm(ks[6], (DEPTH, D), 0.02),
        "w_in": nrm(ks[7], (DEPTH, D, D_IN), D ** -0.5),
        "qn_a": 1.0 + nrm(ks[8], (DEPTH, HEAD_DIM), 0.02),
        "kn_a": 1.0 + nrm(ks[9], (DEPTH, HEAD_DIM), 0.02),
        "qn_b": 1.0 + nrm(ks[10], (DEPTH, HEAD_DIM), 0.02),
        "kn_b": 1.0 + nrm(ks[11], (DEPTH, HEAD_DIM), 0.02),
        "lam_q1": nrm(ks[12], (DEPTH, HEAD_DIM), 0.1),
        "lam_k1": nrm(ks[13], (DEPTH, HEAD_DIM), 0.1),
        "lam_q2": nrm(ks[14], (DEPTH, HEAD_DIM), 0.1),
        "lam_k2": nrm(ks[15], (DEPTH, HEAD_DIM), 0.1),
        "subln_g": 1.0 + nrm(ks[16], (DEPTH, 2 * HEAD_DIM), 0.02),
        "w_proj_a": nrm(ks[17], (DEPTH, BR_A, D), BR_A ** -0.5),
        "w_proj_b": nrm(ks[18], (DEPTH, BR_B, D), BR_B ** -0.5),
        "w_out": nrm(ks[19], (DEPTH, D, D), D ** -0.5),
    }


def reference(x_prompt, x_sample, c_prompt, c_sample, w_ada, b_ada, norm_g, w_in,
              qn_a, kn_a, qn_b, kn_b, lam_q1, lam_k1, lam_q2, lam_k2, subln_g,
              w_proj_a, w_proj_b, w_out):
    yp = x_prompt
    ys = x_sample
    for i in range(DEPTH):
        args = (w_ada[i], b_ada[i], norm_g[i], w_in[i], qn_a[i], kn_a[i], qn_b[i], kn_b[i],
                lam_q1[i], lam_k1[i], lam_q2[i], lam_k2[i], subln_g[i],
                w_proj_a[i], w_proj_b[i], w_out[i])
        yp = encoder_layer(yp, c_prompt, i, *args)
        ys = encoder_layer(ys, c_sample, i, *args)
    y_prompt = yp
    y_sample = ys
    return (y_prompt, y_sample)
```

```python
import functools
import math

import jax
import jax.numpy as jnp
from jax import lax
from jax.experimental import pallas as pl
from jax.experimental.pallas import tpu as pltpu

F32 = jnp.float32
BF16 = jnp.bfloat16

D_MODEL = 1024
SEQ = 8192
GRID_W = 64
HEAD_DIM = 64
HQ_A = 8
HKV_A = 2
GROUP_A = HQ_A // HKV_A
AX_DIM = HEAD_DIM // 2
AX_THETA = 10000.0
A_Q = HQ_A * HEAD_DIM
A_KV = HKV_A * HEAD_DIM
H_B = 4
B_QK = H_B * 2 * HEAD_DIM
B_V = H_B * 2 * HEAD_DIM
ROT_DIM = HEAD_DIM // 4
ROPE_THETA = 500000.0
EPS = 1e-6
LOG2E = math.log2(math.e)
Q_SCALE = LOG2E / math.sqrt(HEAD_DIM)
LAM_INIT = 0.8 - 0.6 * math.exp(-0.3 * 0)

_O_QA = 0
_O_KA = _O_QA + A_Q
_O_VA = _O_KA + A_KV
_O_ZA = _O_VA + A_KV
_O_QB = _O_ZA + A_Q
_O_KB = _O_QB + B_QK
_O_VB = _O_KB + B_QK
_O_ZB = _O_VB + B_V
_O_GA = _O_ZB + B_V
_O_GB = _O_GA + D_MODEL
_O_END = _O_GB + D_MODEL

_R_QA = 0
_R_VA = _R_QA + A_Q
_R_QB = _R_VA + A_KV
_R_VB = _R_QB + B_QK
_R_KA = _R_VB + B_V
_R_KB = _R_KA + A_KV
_R_END = _R_KB + B_QK

TS_IN = 512
TS_OUT = 256
TQ = 512
KV_SUB = 256
KV_CHUNKS_PER_ITER = 2
VMEM_LIMIT = 56 * 1024 * 1024
SAFE_SCORE_BOUND = 40.0
NEG_BIG = -1e30


def _nt_dot(a, b):
    return lax.dot_general(a, b, (((1,), (1,)), ((), ())), preferred_element_type=F32)


def _sigmoid(x):
    return 1.0 / (1.0 + jnp.exp(-x))


def _modulated_norm(x, g, mod):
    ms = jnp.mean(x * x, axis=-1, keepdims=True)
    y = x * lax.rsqrt(ms + EPS) * g
    return y * (1.0 + mod[1:2]) + mod[0:1]


def _ada_kernel(c_ref, w_ref, b_ref, o_ref):
    c = c_ref[...]
    s = (c * _sigmoid(c)).astype(BF16)
    o_ref[...] = jnp.dot(s, w_ref[...], preferred_element_type=F32) + b_ref[...]


def _ada_mod(c, w_ada_bf, b_ada):
    nb = c.shape[0]
    return pl.pallas_call(
        _ada_kernel,
        out_shape=jax.ShapeDtypeStruct((nb, 3 * D_MODEL), F32),
        compiler_params=pltpu.CompilerParams(vmem_limit_bytes=VMEM_LIMIT),
        name="ada_mod",
    )(c, w_ada_bf, b_ada)


def _norm_rope_T(rows, n_heads, c_tab, s_tab, perm_slices):
    ts = rows.shape[-1]
    v = rows.reshape(n_heads, HEAD_DIM, ts)
    ms = jnp.mean(v * v, axis=1, keepdims=True)
    vh = v * lax.rsqrt(ms + EPS)
    partner = jnp.concatenate([vh[:, a:b] for a, b in perm_slices], axis=1)
    return vh * c_tab[None] + partner * s_tab[None]


_PERM_A = ((16, 32), (0, 16), (48, 64), (32, 48))
_PERM_B = ((8, 16), (0, 8), (16, 64))


def _inproj_kernel(x_ref, mod_ref, g_ref, wT_ref, tab_ref, eye_ref,
                   qaT_ref, vaT_ref, qbT_ref, vbT_ref, ka_ref, kb_ref):
    h = _modulated_norm(x_ref[0], g_ref[...], mod_ref[0]).astype(BF16)
    pT = _nt_dot(wT_ref[...], h)
    qaT = _norm_rope_T(pT[_R_QA:_R_VA], HQ_A, tab_ref[0], tab_ref[1], _PERM_A)
    kaT = _norm_rope_T(pT[_R_KA:_R_KB], HKV_A, tab_ref[2], tab_ref[3], _PERM_A)
    qbT = _norm_rope_T(pT[_R_QB:_R_VB], 2 * H_B, tab_ref[4], tab_ref[5], _PERM_B)
    kbT = _norm_rope_T(pT[_R_KB:_R_END], 2 * H_B, tab_ref[6], tab_ref[7], _PERM_B)
    qaT_ref[0] = qaT.astype(BF16)
    qbT_ref[0] = qbT.astype(BF16)
    ts = h.shape[0]
    vaT_ref[0, :, 0] = pT[_R_VA:_R_QB].reshape(HKV_A, HEAD_DIM, ts).astype(BF16)
    vbT_ref[0, :, 0] = pT[_R_VB:_R_KA].reshape(H_B, 2 * HEAD_DIM, ts).astype(BF16)
    eye = eye_ref[...]
    kaT = kaT.astype(BF16)
    kbT = kbT.astype(BF16)
    for hh in range(HKV_A):
        ka_ref[0, hh] = _nt_dot(eye, kaT[hh]).astype(BF16)
    for hh in range(2 * H_B):
        kb_ref[0, hh] = _nt_dot(eye, kbT[hh]).astype(BF16)


def _inproj(x, mod3, norm_g, wT_att, tables, eye):
    nb, s, d = x.shape
    nt = s // TS_IN
    out_shape = (
        jax.ShapeDtypeStruct((nb, HQ_A, HEAD_DIM, s), BF16),
        jax.ShapeDtypeStruct((nb, HKV_A, nt, HEAD_DIM, TS_IN), BF16),
        jax.ShapeDtypeStruct((nb, 2 * H_B, HEAD_DIM, s), BF16),
        jax.ShapeDtypeStruct((nb, H_B, nt, 2 * HEAD_DIM, TS_IN), BF16),
        jax.ShapeDtypeStruct((nb, HKV_A, s, HEAD_DIM), BF16),
        jax.ShapeDtypeStruct((nb, 2 * H_B, s, HEAD_DIM), BF16),
    )
    const = pl.Buffered(1)
    in_specs = [
        pl.BlockSpec((1, TS_IN, d), lambda t, b: (b, t, 0)),
        pl.BlockSpec((1, 3, d), lambda t, b: (b, 0, 0)),
        pl.BlockSpec((1, d), lambda t, b: (0, 0), pipeline_mode=const),
        pl.BlockSpec((_R_END, d), lambda t, b: (0, 0), pipeline_mode=const),
        pl.BlockSpec((8, HEAD_DIM, TS_IN), lambda t, b: (0, 0, t)),
        pl.BlockSpec((TS_IN, TS_IN), lambda t, b: (0, 0), pipeline_mode=const),
    ]
    out_specs = (
        pl.BlockSpec((1, HQ_A, HEAD_DIM, TS_IN), lambda t, b: (b, 0, 0, t)),
        pl.BlockSpec((1, HKV_A, 1, HEAD_DIM, TS_IN), lambda t, b: (b, 0, t, 0, 0)),
        pl.BlockSpec((1, 2 * H_B, HEAD_DIM, TS_IN), lambda t, b: (b, 0, 0, t)),
        pl.BlockSpec((1, H_B, 1, 2 * HEAD_DIM, TS_IN), lambda t, b: (b, 0, t, 0, 0)),
        pl.BlockSpec((1, HKV_A, TS_IN, HEAD_DIM), lambda t, b: (b, 0, t, 0)),
        pl.BlockSpec((1, 2 * H_B, TS_IN, HEAD_DIM), lambda t, b: (b, 0, t, 0)),
    )
    return pl.pallas_call(
        _inproj_kernel,
        grid=(nt, nb),
        in_specs=in_specs,
        out_specs=out_specs,
        out_shape=out_shape,
        compiler_params=pltpu.CompilerParams(
            dimension_semantics=("arbitrary", "arbitrary"), vmem_limit_bytes=VMEM_LIMIT),
        name="in_proj",
    )(x, mod3, norm_g, wT_att, tables, eye)


def _softmax_head(qT, k_ref, v_chunk, acc_ref, m_ref, l_ref, *, online):
    tq = qT.shape[-1]
    s_len = k_ref.shape[0]
    acc_ref[...] = jnp.zeros(acc_ref.shape, F32)
    l_ref[...] = jnp.zeros(l_ref.shape, F32)
    if online:
        m_ref[...] = jnp.full(m_ref.shape, NEG_BIG, F32)
    keys_per_iter = TS_IN * KV_CHUNKS_PER_ITER

    def scores(c, w):
        off = pl.multiple_of(c * TS_IN + w * KV_SUB, KV_SUB)
        return jnp.dot(k_ref[pl.ds(off, KV_SUB), :], qT, preferred_element_type=F32)

    def body(j, carry):
        if online:
            for u in range(KV_CHUNKS_PER_ITER):
                c = j * KV_CHUNKS_PER_ITER + u
                vc = v_chunk(c)
                for w in range(TS_IN // KV_SUB):
                    s = scores(c, w)
                    m_prev = m_ref[...]
                    m_new = jnp.maximum(m_prev, jnp.max(s, axis=0, keepdims=True))
                    alpha = jnp.exp2(m_prev - m_new)
                    p = jnp.exp2(s - m_new)
                    l_ref[...] = alpha * l_ref[...] + p.reshape(KV_SUB // 8, 8, tq).sum(axis=0)
                    pv = jnp.dot(vc[:, w * KV_SUB:(w + 1) * KV_SUB], p.astype(BF16),
                                 preferred_element_type=F32)
                    acc_ref[...] = alpha * acc_ref[...] + pv
                    m_ref[...] = m_new
        else:
            l_part = None
            pv_part = None
            for u in range(KV_CHUNKS_PER_ITER):
                c = j * KV_CHUNKS_PER_ITER + u
                vc = v_chunk(c)
                for w in range(TS_IN // KV_SUB):
                    p = jnp.exp2(scores(c, w))
                    lp = p.reshape(KV_SUB // 8, 8, tq).sum(axis=0)
                    pv = jnp.dot(vc[:, w * KV_SUB:(w + 1) * KV_SUB], p.astype(BF16),
                                 preferred_element_type=F32)
                    l_part = lp if l_part is None else l_part + lp
                    pv_part = pv if pv_part is None else pv_part + pv
            l_ref[...] += l_part
            acc_ref[...] += pv_part
        return carry

    lax.fori_loop(0, s_len // keys_per_iter, body, 0)


def _attn_a_kernel(qT_ref, k_ref, vT_ref, o_ref, oT_ref, acc_ref, m_ref, l_ref, *, online):
    for r in range(GROUP_A):
        _softmax_head(qT_ref[0, r], k_ref.at[0, 0], lambda c: vT_ref[0, 0, c],
                      acc_ref, m_ref, l_ref, online=online)
        l = jnp.sum(l_ref[...], axis=0, keepdims=True)
        oT_ref[r * HEAD_DIM:(r + 1) * HEAD_DIM, :] = acc_ref[...] * (1.0 / l)
    o_ref[0] = oT_ref[...].T


def _attn_b_kernel(lam_ref, qT_ref, k_ref, vT_ref, o_ref, oT_ref, acc_ref, m_ref, l_ref, *, online):
    lv = lam_ref[...]
    lam = (jnp.exp(jnp.sum(lv[0:1] * lv[1:2], axis=1, keepdims=True))
           - jnp.exp(jnp.sum(lv[2:3] * lv[3:4], axis=1, keepdims=True)) + LAM_INIT)
    for comp in range(2):
        _softmax_head(qT_ref[0, comp], k_ref.at[0, comp], lambda c: vT_ref[0, 0, c],
                      acc_ref, m_ref, l_ref, online=online)
        l = jnp.sum(l_ref[...], axis=0, keepdims=True)
        o = acc_ref[...] * (1.0 / l)
        if comp == 0:
            oT_ref[...] = o
        else:
            oT_ref[...] = oT_ref[...] - lam * o
    o_ref[0] = oT_ref[...].T


def _attn_a(qaT, ka, vaT, *, online):
    nb, _, _, s = qaT.shape
    nc = vaT.shape[2]
    width = GROUP_A * HEAD_DIM
    return pl.pallas_call(
        functools.partial(_attn_a_kernel, online=online),
        grid=(nb, HKV_A, s // TQ),
        in_specs=[
            pl.BlockSpec((1, GROUP_A, HEAD_DIM, TQ), lambda b, g, i: (b, g, 0, i)),
            pl.BlockSpec((1, 1, s, HEAD_DIM), lambda b, g, i: (b, g, 0, 0)),
            pl.BlockSpec((1, 1, nc, HEAD_DIM, TS_IN), lambda b, g, i: (b, g, 0, 0, 0)),
        ],
        out_specs=pl.BlockSpec((1, TQ, width), lambda b, g, i: (b, i, g)),
        out_shape=jax.ShapeDtypeStruct((nb, s, A_Q), F32),
        scratch_shapes=[
            pltpu.VMEM((width, TQ), F32),
            pltpu.VMEM((HEAD_DIM, TQ), F32),
            pltpu.VMEM((1, TQ), F32),
            pltpu.VMEM((8, TQ), F32),
        ],
        compiler_params=pltpu.CompilerParams(
            dimension_semantics=("arbitrary", "arbitrary", "arbitrary"), vmem_limit_bytes=VMEM_LIMIT),
        name="attn_gqa_online" if online else "attn_gqa",
    )(qaT, ka, vaT)


def _attn_b(lam_vecs, qbT, kb, vbT, *, online):
    nb, _, _, s = qbT.shape
    nc = vbT.shape[2]
    dv = 2 * HEAD_DIM
    return pl.pallas_call(
        functools.partial(_attn_b_kernel, online=online),
        grid=(nb, H_B, s // TQ),
        in_specs=[
            pl.BlockSpec((4, HEAD_DIM), lambda b, h, i: (0, 0)),
            pl.BlockSpec((1, 2, HEAD_DIM, TQ), lambda b, h, i: (b, h, 0, i)),
            pl.BlockSpec((1, 2, s, HEAD_DIM), lambda b, h, i: (b, h, 0, 0)),
            pl.BlockSpec((1, 1, nc, dv, TS_IN), lambda b, h, i: (b, h, 0, 0, 0)),
        ],
        out_specs=pl.BlockSpec((1, TQ, dv), lambda b, h, i: (b, i, h)),
        out_shape=jax.ShapeDtypeStruct((nb, s, B_V), F32),
        scratch_shapes=[
            pltpu.VMEM((dv, TQ), F32),
            pltpu.VMEM((dv, TQ), F32),
            pltpu.VMEM((1, TQ), F32),
            pltpu.VMEM((8, TQ), F32),
        ],
        compiler_params=pltpu.CompilerParams(
            dimension_semantics=("arbitrary", "arbitrary", "arbitrary"), vmem_limit_bytes=VMEM_LIMIT),
        name="attn_diff_online" if online else "attn_diff",
    )(lam_vecs, qbT, kb, vbT)


def _out_kernel(x_ref, mod_ref, g_ref, oa_ref, ob_ref, wg_ref, wpa_ref, wpb_ref, wo_ref, sg_ref, y_ref):
    x = x_ref[0]
    mod = mod_ref[0]
    h = _modulated_norm(x, g_ref[...], mod).astype(BF16)
    gates = jnp.dot(h, wg_ref[...], preferred_element_type=F32)
    za = gates[:, 0:A_Q]
    zb = gates[:, A_Q:A_Q + B_V]
    ga = gates[:, A_Q + B_V:A_Q + B_V + D_MODEL]
    gb = gates[:, A_Q + B_V + D_MODEL:]
    a = oa_ref[0] * (za * _sigmoid(za))
    pa = jnp.dot(a.astype(BF16), wpa_ref[...], preferred_element_type=F32)
    dv = 2 * HEAD_DIM
    ob = ob_ref[0]
    sg = sg_ref[...] * (1.0 - LAM_INIT)
    normed = []
    for hh in range(H_B):
        oh = ob[:, hh * dv:(hh + 1) * dv]
        ms = jnp.mean(oh * oh, axis=-1, keepdims=True)
        normed.append(oh * lax.rsqrt(ms + EPS) * sg)
    bn = jnp.concatenate(normed, axis=-1) * (zb * _sigmoid(zb))
    pb = jnp.dot(bn.astype(BF16), wpb_ref[...], preferred_element_type=F32)
    merged = _sigmoid(ga) * pa + _sigmoid(gb) * pb
    out = jnp.dot(merged.astype(BF16), wo_ref[...], preferred_element_type=F32)
    y_ref[0] = x + mod[2:3] * out


def _out_stage(x, mod3, norm_g, oa, ob, w_gates, w_pa, w_pb, w_out, subln_g):
    nb, s, d = x.shape
    const = pl.Buffered(1)
    n_gate = w_gates.shape[1]
    return pl.pallas_call(
        _out_kernel,
        grid=(nb, s // TS_OUT),
        in_specs=[
            pl.BlockSpec((1, TS_OUT, d), lambda b, t: (b, t, 0)),
            pl.BlockSpec((1, 3, d), lambda b, t: (b, 0, 0)),
            pl.BlockSpec((1, d), lambda b, t: (0, 0), pipeline_mode=const),
            pl.BlockSpec((1, TS_OUT, A_Q), lambda b, t: (b, t, 0)),
            pl.BlockSpec((1, TS_OUT, B_V), lambda b, t: (b, t, 0)),
            pl.BlockSpec((d, n_gate), lambda b, t: (0, 0), pipeline_mode=const),
            pl.BlockSpec((A_Q, d), lambda b, t: (0, 0), pipeline_mode=const),
            pl.BlockSpec((B_V, d), lambda b, t: (0, 0), pipeline_mode=const),
            pl.BlockSpec((d, d), lambda b, t: (0, 0), pipeline_mode=const),
            pl.BlockSpec((1, 2 * HEAD_DIM), lambda b, t: (0, 0), pipeline_mode=const),
        ],
        out_specs=pl.BlockSpec((1, TS_OUT, d), lambda b, t: (b, t, 0)),
        out_shape=jax.ShapeDtypeStruct((nb, s, d), F32),
        compiler_params=pltpu.CompilerParams(
            dimension_semantics=("arbitrary", "arbitrary"), vmem_limit_bytes=VMEM_LIMIT),
        name="out_stage",
    )(x, mod3, norm_g, oa, ob, w_gates, w_pa, w_pb, w_out, subln_g)


def _rope_coeff_tables(qn_a, kn_a, qn_b, kn_b, s):
    t = jnp.arange(s)
    row = (t // GRID_W).astype(F32)
    col = (t % GRID_W).astype(F32)
    inv_ax = AX_THETA ** (-jnp.arange(0, AX_DIM, 2, dtype=F32) / AX_DIM)
    ang_r = inv_ax[:, None] * row[None, :]
    ang_c = inv_ax[:, None] * col[None, :]
    ang_a = jnp.concatenate([ang_r, ang_r, ang_c, ang_c], axis=0)
    cos_a, sin_a = jnp.cos(ang_a), jnp.sin(ang_a)
    half = AX_DIM // 2
    sign_a = jnp.concatenate([-jnp.ones(half), jnp.ones(half), -jnp.ones(half), jnp.ones(half)]).astype(F32)
    perm_a = jnp.concatenate([jnp.arange(a, b) for a, b in _PERM_A])

    inv_p = ROPE_THETA ** (-jnp.arange(0, ROT_DIM, 2, dtype=F32) / ROT_DIM)
    ang_p = inv_p[:, None] * t.astype(F32)[None, :]
    ang_p = jnp.concatenate([ang_p, ang_p], axis=0)
    rest = HEAD_DIM - ROT_DIM
    cos_b = jnp.concatenate([jnp.cos(ang_p), jnp.ones((rest, s), F32)], axis=0)
    sin_b = jnp.concatenate([jnp.sin(ang_p), jnp.zeros((rest, s), F32)], axis=0)
    hb = ROT_DIM // 2
    sign_b = jnp.concatenate([-jnp.ones(hb), jnp.ones(hb), jnp.zeros(rest)]).astype(F32)
    perm_b = jnp.concatenate([jnp.arange(a, b) for a, b in _PERM_B])

    def pair(g, cos, sin, sign, perm, scale):
        g = g.astype(F32)
        return [scale * g[:, None] * cos, scale * (sign * g[perm])[:, None] * sin]

    tabs = (pair(qn_a, cos_a, sin_a, sign_a, perm_a, Q_SCALE) + pair(kn_a, cos_a, sin_a, sign_a, perm_a, 1.0)
            + pair(qn_b, cos_b, sin_b, sign_b, perm_b, Q_SCALE) + pair(kn_b, cos_b, sin_b, sign_b, perm_b, 1.0))
    return jnp.stack(tabs, axis=0)


def _layer(x, c, wts, tables, eye, safe):
    (w_ada_bf, b_ada, norm_g, wT_att, w_gates, w_pa, w_pb, w_out, subln_g, lam_vecs) = wts
    nb = x.shape[0]
    mod3 = _ada_mod(c, w_ada_bf, b_ada).reshape(nb, 3, D_MODEL)
    qaT, vaT, qbT, vbT, ka, kb = _inproj(x, mod3, norm_g, wT_att, tables, eye)

    def attend(online):
        def run(ops):
            qaT_, ka_, vaT_, qbT_, kb_, vbT_ = ops
            return (_attn_a(qaT_, ka_, vaT_, online=online),
                    _attn_b(lam_vecs, qbT_, kb_, vbT_, online=online))
        return run

    oa, ob = lax.cond(safe, attend(False), attend(True), (qaT, ka, vaT, qbT, kb, vbT))
    return _out_stage(x, mod3, norm_g, oa, ob, w_gates, w_pa, w_pb, w_out, subln_g)


def kernel(x_prompt, x_sample, c_prompt, c_sample, w_ada, b_ada, norm_g, w_in, qn_a, kn_a, qn_b, kn_b,
           lam_q1, lam_k1, lam_q2, lam_k2, subln_g, w_proj_a, w_proj_b, w_out):
    w = w_in[0]
    cols = lambda a, b: w[:, a:b]
    wT_att = jnp.concatenate(
        [cols(_O_QA, _O_KA), cols(_O_VA, _O_ZA), cols(_O_QB, _O_KB), cols(_O_VB, _O_ZB),
         cols(_O_KA, _O_VA), cols(_O_KB, _O_VB)], axis=1).T.astype(BF16)
    w_gates = jnp.concatenate(
        [cols(_O_ZA, _O_QB), cols(_O_ZB, _O_GA), cols(_O_GA, _O_END)], axis=1).astype(BF16)
    wts = (w_ada[0].astype(BF16), b_ada[0][None, :], norm_g[0][None, :], wT_att, w_gates,
           w_proj_a[0].astype(BF16), w_proj_b[0].astype(BF16), w_out[0].astype(BF16),
           subln_g[0][None, :], jnp.stack([lam_q1[0], lam_k1[0], lam_q2[0], lam_k2[0]], axis=0).astype(F32))
    tables = _rope_coeff_tables(qn_a[0], kn_a[0], qn_b[0], kn_b[0], SEQ)
    eye = jnp.eye(TS_IN, dtype=BF16)
    bound_a = math.sqrt(HEAD_DIM) * jnp.max(jnp.abs(qn_a[0])) * jnp.max(jnp.abs(kn_a[0]))
    bound_b = math.sqrt(HEAD_DIM) * jnp.max(jnp.abs(qn_b[0])) * jnp.max(jnp.abs(kn_b[0]))
    safe = jnp.maximum(bound_a, bound_b) <= SAFE_SCORE_BOUND
    y_prompt = _layer(x_prompt, c_prompt, wts, tables, eye, safe)
    y_sample = _layer(x_sample, c_sample, wts, tables, eye, safe)
    return (y_prompt, y_sample)
```

```python
import functools
import math

import jax
import jax.numpy as jnp
from jax import lax
from jax.experimental import pallas as pl
from jax.experimental.pallas import tpu as pltpu

F32 = jnp.float32
BF16 = jnp.bfloat16

D_MODEL = 1024
SEQ = 8192
GRID_W = 64
HEAD_DIM = 64
HQ_A = 8
HKV_A = 2
GROUP_A = HQ_A // HKV_A
AX_DIM = HEAD_DIM // 2
AX_THETA = 10000.0
A_Q = HQ_A * HEAD_DIM
A_KV = HKV_A * HEAD_DIM
H_B = 4
B_QK = H_B * 2 * HEAD_DIM
B_V = H_B * 2 * HEAD_DIM
ROT_DIM = HEAD_DIM // 4
ROPE_THETA = 500000.0
EPS = 1e-6
LOG2E = math.log2(math.e)
Q_SCALE = LOG2E / math.sqrt(HEAD_DIM)
LAM_INIT = 0.8 - 0.6 * math.exp(-0.3 * 0)

_O_QA = 0
_O_KA = _O_QA + A_Q
_O_VA = _O_KA + A_KV
_O_ZA = _O_VA + A_KV
_O_QB = _O_ZA + A_Q
_O_KB = _O_QB + B_QK
_O_VB = _O_KB + B_QK
_O_ZB = _O_VB + B_V
_O_GA = _O_ZB + B_V
_O_GB = _O_GA + D_MODEL
_O_END = _O_GB + D_MODEL

_R_QA = 0
_R_VA = _R_QA + A_Q
_R_QB = _R_VA + A_KV
_R_VB = _R_QB + B_QK
_R_KA = _R_VB + B_V
_R_KB = _R_KA + A_KV
_R_END = _R_KB + B_QK

TS_IN = 512
TS_OUT = 256
TQ = 512
KV_SUB = 256
KV_CHUNKS_PER_ITER = 4
SCORE_LOOKAHEAD = 3
VMEM_LIMIT = 56 * 1024 * 1024
SAFE_SCORE_BOUND = 40.0
NEG_BIG = -1e30


def _nt_dot(a, b):
    return lax.dot_general(a, b, (((1,), (1,)), ((), ())), preferred_element_type=F32)


def _sigmoid(x):
    return 1.0 / (1.0 + jnp.exp(-x))


def _modulated_norm(x, g, mod):
    ms = jnp.mean(x * x, axis=-1, keepdims=True)
    y = x * lax.rsqrt(ms + EPS) * g
    return y * (1.0 + mod[1:2]) + mod[0:1]


def _ada_kernel(c_ref, w_ref, b_ref, o_ref):
    c = c_ref[...]
    s = (c * _sigmoid(c)).astype(BF16)
    o_ref[...] = jnp.dot(s, w_ref[...], preferred_element_type=F32) + b_ref[...]


def _ada_mod(c, w_ada_bf, b_ada):
    nb = c.shape[0]
    return pl.pallas_call(
        _ada_kernel,
        out_shape=jax.ShapeDtypeStruct((nb, 3 * D_MODEL), F32),
        compiler_params=pltpu.CompilerParams(vmem_limit_bytes=VMEM_LIMIT),
        name="ada_mod",
    )(c, w_ada_bf, b_ada)


def _norm_rope_T(rows, n_heads, c_tab, s_tab, perm_slices):
    ts = rows.shape[-1]
    v = rows.reshape(n_heads, HEAD_DIM, ts)
    ms = jnp.mean(v * v, axis=1, keepdims=True)
    vh = v * lax.rsqrt(ms + EPS)
    partner = jnp.concatenate([vh[:, a:b] for a, b in perm_slices], axis=1)
    return vh * c_tab[None] + partner * s_tab[None]


_PERM_A = ((16, 32), (0, 16), (48, 64), (32, 48))
_PERM_B = ((8, 16), (0, 8), (16, 64))


def _inproj_kernel(x_ref, mod_ref, g_ref, wT_ref, tab_ref, eye_ref,
                   qaT_ref, vaT_ref, qbT_ref, vbT_ref, ka_ref, kb_ref):
    h = _modulated_norm(x_ref[0], g_ref[...], mod_ref[0]).astype(BF16)
    pT = _nt_dot(wT_ref[...], h)
    qaT = _norm_rope_T(pT[_R_QA:_R_VA], HQ_A, tab_ref[0], tab_ref[1], _PERM_A)
    kaT = _norm_rope_T(pT[_R_KA:_R_KB], HKV_A, tab_ref[2], tab_ref[3], _PERM_A)
    qbT = _norm_rope_T(pT[_R_QB:_R_VB], 2 * H_B, tab_ref[4], tab_ref[5], _PERM_B)
    kbT = _norm_rope_T(pT[_R_KB:_R_END], 2 * H_B, tab_ref[6], tab_ref[7], _PERM_B)
    qaT_ref[0] = qaT.astype(BF16)
    qbT_ref[0] = qbT.astype(BF16)
    ts = h.shape[0]
    vaT_ref[0, :, 0] = pT[_R_VA:_R_QB].reshape(HKV_A, HEAD_DIM, ts).astype(BF16)
    vbT_ref[0, :, 0] = pT[_R_VB:_R_KA].reshape(H_B, 2 * HEAD_DIM, ts).astype(BF16)
    eye = eye_ref[...]
    kaT = kaT.astype(BF16)
    kbT = kbT.astype(BF16)
    for hh in range(HKV_A):
        ka_ref[0, hh] = _nt_dot(eye, kaT[hh]).astype(BF16)
    for hh in range(2 * H_B):
        kb_ref[0, hh] = _nt_dot(eye, kbT[hh]).astype(BF16)


def _inproj(x, mod3, norm_g, wT_att, tables, eye):
    nb, s, d = x.shape
    nt = s // TS_IN
    out_shape = (
        jax.ShapeDtypeStruct((nb, HQ_A, HEAD_DIM, s), BF16),
        jax.ShapeDtypeStruct((nb, HKV_A, nt, HEAD_DIM, TS_IN), BF16),
        jax.ShapeDtypeStruct((nb, 2 * H_B, HEAD_DIM, s), BF16),
        jax.ShapeDtypeStruct((nb, H_B, nt, 2 * HEAD_DIM, TS_IN), BF16),
        jax.ShapeDtypeStruct((nb, HKV_A, s, HEAD_DIM), BF16),
        jax.ShapeDtypeStruct((nb, 2 * H_B, s, HEAD_DIM), BF16),
    )
    const = pl.Buffered(1)
    in_specs = [
        pl.BlockSpec((1, TS_IN, d), lambda t, b: (b, t, 0)),
        pl.BlockSpec((1, 3, d), lambda t, b: (b, 0, 0)),
        pl.BlockSpec((1, d), lambda t, b: (0, 0), pipeline_mode=const),
        pl.BlockSpec((_R_END, d), lambda t, b: (0, 0), pipeline_mode=const),
        pl.BlockSpec((8, HEAD_DIM, TS_IN), lambda t, b: (0, 0, t)),
        pl.BlockSpec((TS_IN, TS_IN), lambda t, b: (0, 0), pipeline_mode=const),
    ]
    out_specs = (
        pl.BlockSpec((1, HQ_A, HEAD_DIM, TS_IN), lambda t, b: (b, 0, 0, t)),
        pl.BlockSpec((1, HKV_A, 1, HEAD_DIM, TS_IN), lambda t, b: (b, 0, t, 0, 0)),
        pl.BlockSpec((1, 2 * H_B, HEAD_DIM, TS_IN), lambda t, b: (b, 0, 0, t)),
        pl.BlockSpec((1, H_B, 1, 2 * HEAD_DIM, TS_IN), lambda t, b: (b, 0, t, 0, 0)),
        pl.BlockSpec((1, HKV_A, TS_IN, HEAD_DIM), lambda t, b: (b, 0, t, 0)),
        pl.BlockSpec((1, 2 * H_B, TS_IN, HEAD_DIM), lambda t, b: (b, 0, t, 0)),
    )
    return pl.pallas_call(
        _inproj_kernel,
        grid=(nt, nb),
        in_specs=in_specs,
        out_specs=out_specs,
        out_shape=out_shape,
        compiler_params=pltpu.CompilerParams(
            dimension_semantics=("arbitrary", "arbitrary"), vmem_limit_bytes=VMEM_LIMIT),
        name="in_proj",
    )(x, mod3, norm_g, wT_att, tables, eye)


def _attend_tile(n_maps, q_of, k_of, v_of, acc_ref, m_ref, l_ref, *, online):
    tq = acc_ref.shape[-1]
    s_len = k_of(0).shape[0]
    acc_ref[...] = jnp.zeros(acc_ref.shape, F32)
    l_ref[...] = jnp.zeros(l_ref.shape, F32)
    if online:
        m_ref[...] = jnp.full(m_ref.shape, NEG_BIG, F32)
    subs = TS_IN // KV_SUB

    def scores(i, c, w):
        off = pl.multiple_of(c * TS_IN + w * KV_SUB, KV_SUB)
        return jnp.dot(k_of(i)[pl.ds(off, KV_SUB), :], q_of(i), preferred_element_type=F32)

    def online_body(j, carry):
        for u in range(KV_CHUNKS_PER_ITER):
            c = j * KV_CHUNKS_PER_ITER + u
            for w in range(subs):
                for i in range(n_maps):
                    s = scores(i, c, w)
                    m_prev = m_ref[i]
                    m_new = jnp.maximum(m_prev, jnp.max(s, axis=0, keepdims=True))
                    alpha = jnp.exp2(m_prev - m_new)
                    p = jnp.exp2(s - m_new)
                    l_ref[i] = alpha * l_ref[i] + p.reshape(KV_SUB // 8, 8, tq).sum(axis=0)
                    pv = jnp.dot(v_of(c, w), p.astype(BF16), preferred_element_type=F32)
                    acc_ref[i] = alpha * acc_ref[i] + pv
                    m_ref[i] = m_new
        return carry

    def raw_body(j, carry):
        items = [(i, u, w) for u in range(KV_CHUNKS_PER_ITER) for w in range(subs) for i in range(n_maps)]
        l_part = [None] * n_maps
        pv_part = [None] * n_maps
        pending = []

        def consume(item, s):
            i, u, w = item
            p = jnp.exp2(s)
            lp = p.reshape(KV_SUB // 8, 8, tq).sum(axis=0)
            pv = jnp.dot(v_of(j * KV_CHUNKS_PER_ITER + u, w), p.astype(BF16), preferred_element_type=F32)
            l_part[i] = lp if l_part[i] is None else l_part[i] + lp
            pv_part[i] = pv if pv_part[i] is None else pv_part[i] + pv

        for item in items:
            i, u, w = item
            pending.append((item, scores(i, j * KV_CHUNKS_PER_ITER + u, w)))
            if len(pending) > SCORE_LOOKAHEAD:
                consume(*pending.pop(0))
        while pending:
            consume(*pending.pop(0))
        for i in range(n_maps):
            l_ref[i] += l_part[i]
            acc_ref[i] += pv_part[i]
        return carry

    lax.fori_loop(0, s_len // (TS_IN * KV_CHUNKS_PER_ITER), online_body if online else raw_body, 0)


def _attn_a_kernel(qT_ref, k_ref, vT_ref, o_ref, acc_ref, m_ref, l_ref, *, online):
    _attend_tile(GROUP_A, lambda i: qT_ref[0, i], lambda i: k_ref.at[0, 0],
                 lambda c, w: vT_ref[0, 0, c, :, w * KV_SUB:(w + 1) * KV_SUB],
                 acc_ref, m_ref, l_ref, online=online)
    l = jnp.sum(l_ref[...], axis=1, keepdims=True)
    oT = acc_ref[...] * (1.0 / l)
    o_ref[0] = oT.reshape(GROUP_A * HEAD_DIM, oT.shape[-1]).T


def _attn_b_kernel(lam_ref, qT_ref, k_ref, vT_ref, o_ref, acc_ref, m_ref, l_ref, *, online):
    lv = lam_ref[...]
    lam = (jnp.exp(jnp.sum(lv[0:1] * lv[1:2], axis=1, keepdims=True))
           - jnp.exp(jnp.sum(lv[2:3] * lv[3:4], axis=1, keepdims=True)) + LAM_INIT)
    _attend_tile(2, lambda i: qT_ref[0, i], lambda i: k_ref.at[0, i],
                 lambda c, w: vT_ref[0, 0, c, :, w * KV_SUB:(w + 1) * KV_SUB],
                 acc_ref, m_ref, l_ref, online=online)
    l = jnp.sum(l_ref[...], axis=1, keepdims=True)
    o = acc_ref[...] * (1.0 / l)
    o_ref[0] = (o[0] - lam * o[1]).T


def _attn_a(qaT, ka, vaT, *, online):
    nb, _, _, s = qaT.shape
    nc = vaT.shape[2]
    width = GROUP_A * HEAD_DIM
    return pl.pallas_call(
        functools.partial(_attn_a_kernel, online=online),
        grid=(nb, HKV_A, s // TQ),
        in_specs=[
            pl.BlockSpec((1, GROUP_A, HEAD_DIM, TQ), lambda b, g, i: (b, g, 0, i)),
            pl.BlockSpec((1, 1, s, HEAD_DIM), lambda b, g, i: (b, g, 0, 0)),
            pl.BlockSpec((1, 1, nc, HEAD_DIM, TS_IN), lambda b, g, i: (b, g, 0, 0, 0)),
        ],
        out_specs=pl.BlockSpec((1, TQ, width), lambda b, g, i: (b, i, g)),
        out_shape=jax.ShapeDtypeStruct((nb, s, A_Q), F32),
        scratch_shapes=[
            pltpu.VMEM((GROUP_A, HEAD_DIM, TQ), F32),
            pltpu.VMEM((GROUP_A, 1, TQ), F32),
            pltpu.VMEM((GROUP_A, 8, TQ), F32),
        ],
        compiler_params=pltpu.CompilerParams(
            dimension_semantics=("arbitrary", "arbitrary", "arbitrary"), vmem_limit_bytes=VMEM_LIMIT),
        name="attn_gqa_online" if online else "attn_gqa",
    )(qaT, ka, vaT)


def _attn_b(lam_vecs, qbT, kb, vbT, *, online):
    nb, _, _, s = qbT.shape
    nc = vbT.shape[2]
    dv = 2 * HEAD_DIM
    return pl.pallas_call(
        functools.partial(_attn_b_kernel, online=online),
        grid=(nb, H_B, s // TQ),
        in_specs=[
            pl.BlockSpec((4, HEAD_DIM), lambda b, h, i: (0, 0)),
            pl.BlockSpec((1, 2, HEAD_DIM, TQ), lambda b, h, i: (b, h, 0, i)),
            pl.BlockSpec((1, 2, s, HEAD_DIM), lambda b, h, i: (b, h, 0, 0)),
            pl.BlockSpec((1, 1, nc, dv, TS_IN), lambda b, h, i: (b, h, 0, 0, 0)),
        ],
        out_specs=pl.BlockSpec((1, TQ, dv), lambda b, h, i: (b, i, h)),
        out_shape=jax.ShapeDtypeStruct((nb, s, B_V), F32),
        scratch_shapes=[
            pltpu.VMEM((2, dv, TQ), F32),
            pltpu.VMEM((2, 1, TQ), F32),
            pltpu.VMEM((2, 8, TQ), F32),
        ],
        compiler_params=pltpu.CompilerParams(
            dimension_semantics=("arbitrary", "arbitrary", "arbitrary"), vmem_limit_bytes=VMEM_LIMIT),
        name="attn_diff_online" if online else "attn_diff",
    )(lam_vecs, qbT, kb, vbT)


def _out_kernel(x_ref, mod_ref, g_ref, oa_ref, ob_ref, wg_ref, wpa_ref, wpb_ref, wo_ref, sg_ref, y_ref):
    x = x_ref[0]
    mod = mod_ref[0]
    h = _modulated_norm(x, g_ref[...], mod).astype(BF16)
    gates = jnp.dot(h, wg_ref[...], preferred_element_type=F32)
    za = gates[:, 0:A_Q]
    zb = gates[:, A_Q:A_Q + B_V]
    ga = gates[:, A_Q + B_V:A_Q + B_V + D_MODEL]
    gb = gates[:, A_Q + B_V + D_MODEL:]
    a = oa_ref[0] * (za * _sigmoid(za))
    pa = jnp.dot(a.astype(BF16), wpa_ref[...], preferred_element_type=F32)
    dv = 2 * HEAD_DIM
    ob = ob_ref[0]
    sg = sg_ref[...] * (1.0 - LAM_INIT)
    normed = []
    for hh in range(H_B):
        oh = ob[:, hh * dv:(hh + 1) * dv]
        ms = jnp.mean(oh * oh, axis=-1, keepdims=True)
        normed.append(oh * lax.rsqrt(ms + EPS) * sg)
    bn = jnp.concatenate(normed, axis=-1) * (zb * _sigmoid(zb))
    pb = jnp.dot(bn.astype(BF16), wpb_ref[...], preferred_element_type=F32)
    merged = _sigmoid(ga) * pa + _sigmoid(gb) * pb
    out = jnp.dot(merged.astype(BF16), wo_ref[...], preferred_element_type=F32)
    y_ref[0] = x + mod[2:3] * out


def _out_stage(x, mod3, norm_g, oa, ob, w_gates, w_pa, w_pb, w_out, subln_g):
    nb, s, d = x.shape
    const = pl.Buffered(1)
    n_gate = w_gates.shape[1]
    return pl.pallas_call(
        _out_kernel,
        grid=(nb, s // TS_OUT),
        in_specs=[
            pl.BlockSpec((1, TS_OUT, d), lambda b, t: (b, t, 0)),
            pl.BlockSpec((1, 3, d), lambda b, t: (b, 0, 0)),
            pl.BlockSpec((1, d), lambda b, t: (0, 0), pipeline_mode=const),
            pl.BlockSpec((1, TS_OUT, A_Q), lambda b, t: (b, t, 0)),
            pl.BlockSpec((1, TS_OUT, B_V), lambda b, t: (b, t, 0)),
            pl.BlockSpec((d, n_gate), lambda b, t: (0, 0), pipeline_mode=const),
            pl.BlockSpec((A_Q, d), lambda b, t: (0, 0), pipeline_mode=const),
            pl.BlockSpec((B_V, d), lambda b, t: (0, 0), pipeline_mode=const),
            pl.BlockSpec((d, d), lambda b, t: (0, 0), pipeline_mode=const),
            pl.BlockSpec((1, 2 * HEAD_DIM), lambda b, t: (0, 0), pipeline_mode=const),
        ],
        out_specs=pl.BlockSpec((1, TS_OUT, d), lambda b, t: (b, t, 0)),
        out_shape=jax.ShapeDtypeStruct((nb, s, d), F32),
        compiler_params=pltpu.CompilerParams(
            dimension_semantics=("arbitrary", "arbitrary"), vmem_limit_bytes=VMEM_LIMIT),
        name="out_stage",
    )(x, mod3, norm_g, oa, ob, w_gates, w_pa, w_pb, w_out, subln_g)


def _rope_coeff_tables(qn_a, kn_a, qn_b, kn_b, s):
    t = jnp.arange(s)
    row = (t // GRID_W).astype(F32)
    col = (t % GRID_W).astype(F32)
    inv_ax = AX_THETA ** (-jnp.arange(0, AX_DIM, 2, dtype=F32) / AX_DIM)
    ang_r = inv_ax[:, None] * row[None, :]
    ang_c = inv_ax[:, None] * col[None, :]
    ang_a = jnp.concatenate([ang_r, ang_r, ang_c, ang_c], axis=0)
    cos_a, sin_a = jnp.cos(ang_a), jnp.sin(ang_a)
    half = AX_DIM // 2
    sign_a = jnp.concatenate([-jnp.ones(half), jnp.ones(half), -jnp.ones(half), jnp.ones(half)]).astype(F32)
    perm_a = jnp.concatenate([jnp.arange(a, b) for a, b in _PERM_A])

    inv_p = ROPE_THETA ** (-jnp.arange(0, ROT_DIM, 2, dtype=F32) / ROT_DIM)
    ang_p = inv_p[:, None] * t.astype(F32)[None, :]
    ang_p = jnp.concatenate([ang_p, ang_p], axis=0)
    rest = HEAD_DIM - ROT_DIM
    cos_b = jnp.concatenate([jnp.cos(ang_p), jnp.ones((rest, s), F32)], axis=0)
    sin_b = jnp.concatenate([jnp.sin(ang_p), jnp.zeros((rest, s), F32)], axis=0)
    hb = ROT_DIM // 2
    sign_b = jnp.concatenate([-jnp.ones(hb), jnp.ones(hb), jnp.zeros(rest)]).astype(F32)
    perm_b = jnp.concatenate([jnp.arange(a, b) for a, b in _PERM_B])

    def pair(g, cos, sin, sign, perm, scale):
        g = g.astype(F32)
        return [scale * g[:, None] * cos, scale * (sign * g[perm])[:, None] * sin]

    tabs = (pair(qn_a, cos_a, sin_a, sign_a, perm_a, Q_SCALE) + pair(kn_a, cos_a, sin_a, sign_a, perm_a, 1.0)
            + pair(qn_b, cos_b, sin_b, sign_b, perm_b, Q_SCALE) + pair(kn_b, cos_b, sin_b, sign_b, perm_b, 1.0))
    return jnp.stack(tabs, axis=0)


def _layer(x, c, wts, tables, eye, safe):
    (w_ada_bf, b_ada, norm_g, wT_att, w_gates, w_pa, w_pb, w_out, subln_g, lam_vecs) = wts
    nb = x.shape[0]
    mod3 = _ada_mod(c, w_ada_bf, b_ada).reshape(nb, 3, D_MODEL)
    qaT, vaT, qbT, vbT, ka, kb = _inproj(x, mod3, norm_g, wT_att, tables, eye)

    def attend(online):
        def run(ops):
            qaT_, ka_, vaT_, qbT_, kb_, vbT_ = ops
            return (_attn_a(qaT_, ka_, vaT_, online=online),
                    _attn_b(lam_vecs, qbT_, kb_, vbT_, online=online))
        return run

    oa, ob = lax.cond(safe, attend(False), attend(True), (qaT, ka, vaT, qbT, kb, vbT))
    return _out_stage(x, mod3, norm_g, oa, ob, w_gates, w_pa, w_pb, w_out, subln_g)


def kernel(x_prompt, x_sample, c_prompt, c_sample, w_ada, b_ada, norm_g, w_in, qn_a, kn_a, qn_b, kn_b,
           lam_q1, lam_k1, lam_q2, lam_k2, subln_g, w_proj_a, w_proj_b, w_out):
    w = w_in[0]
    cols = lambda a, b: w[:, a:b]
    wT_att = jnp.concatenate(
        [cols(_O_QA, _O_KA), cols(_O_VA, _O_ZA), cols(_O_QB, _O_KB), cols(_O_VB, _O_ZB),
         cols(_O_KA, _O_VA), cols(_O_KB, _O_VB)], axis=1).T.astype(BF16)
    w_gates = jnp.concatenate(
        [cols(_O_ZA, _O_QB), cols(_O_ZB, _O_GA), cols(_O_GA, _O_END)], axis=1).astype(BF16)
    wts = (w_ada[0].astype(BF16), b_ada[0][None, :], norm_g[0][None, :], wT_att, w_gates,
           w_proj_a[0].astype(BF16), w_proj_b[0].astype(BF16), w_out[0].astype(BF16),
           subln_g[0][None, :], jnp.stack([lam_q1[0], lam_k1[0], lam_q2[0], lam_k2[0]], axis=0).astype(F32))
    tables = _rope_coeff_tables(qn_a[0], kn_a[0], qn_b[0], kn_b[0], SEQ)
    eye = jnp.eye(TS_IN, dtype=BF16)
    bound_a = math.sqrt(HEAD_DIM) * jnp.max(jnp.abs(qn_a[0])) * jnp.max(jnp.abs(kn_a[0]))
    bound_b = math.sqrt(HEAD_DIM) * jnp.max(jnp.abs(qn_b[0])) * jnp.max(jnp.abs(kn_b[0]))
    safe = jnp.maximum(bound_a, bound_b) <= SAFE_SCORE_BOUND
    y_prompt = _layer(x_prompt, c_prompt, wts, tables, eye, safe)
    y_sample = _layer(x_sample, c_sample, wts, tables, eye, safe)
    return (y_prompt, y_sample)
```

```python
import functools
import math

import jax
import jax.numpy as jnp
from jax import lax
from jax.experimental import pallas as pl
from jax.experimental.pallas import tpu as pltpu

F32 = jnp.float32
BF16 = jnp.bfloat16

D_MODEL = 1024
SEQ = 8192
GRID_W = 64
HEAD_DIM = 64
HQ_A = 8
HKV_A = 2
GROUP_A = HQ_A // HKV_A
AX_DIM = HEAD_DIM // 2
AX_THETA = 10000.0
A_Q = HQ_A * HEAD_DIM
A_KV = HKV_A * HEAD_DIM
H_B = 4
B_QK = H_B * 2 * HEAD_DIM
B_V = H_B * 2 * HEAD_DIM
ROT_DIM = HEAD_DIM // 4
ROPE_THETA = 500000.0
EPS = 1e-6
LOG2E = math.log2(math.e)
Q_SCALE = LOG2E / math.sqrt(HEAD_DIM)
LAM_INIT = 0.8 - 0.6 * math.exp(-0.3 * 0)

_O_QA = 0
_O_KA = _O_QA + A_Q
_O_VA = _O_KA + A_KV
_O_ZA = _O_VA + A_KV
_O_QB = _O_ZA + A_Q
_O_KB = _O_QB + B_QK
_O_VB = _O_KB + B_QK
_O_ZB = _O_VB + B_V
_O_GA = _O_ZB + B_V
_O_GB = _O_GA + D_MODEL
_O_END = _O_GB + D_MODEL

_R_QA = 0
_R_QB = _R_QA + A_Q
_R_KB = _R_QB + B_QK
_R_KA = _R_KB + B_QK
_R_VB = _R_KA + A_KV
_R_VA = _R_VB + B_V
_R_END = _R_VA + A_KV

TS_IN = 512
IN_SUB = 256
IN_LOOKAHEAD = 2
TS_OUT = 256
TQ = 512
KV_SUB = 256
KV_CHUNKS_PER_ITER = 16
SCORE_LOOKAHEAD = 3
VMEM_LIMIT = 56 * 1024 * 1024
SAFE_SCORE_BOUND = 40.0
NEG_BIG = -1e30


def _nt_dot(a, b):
    return lax.dot_general(a, b, (((1,), (1,)), ((), ())), preferred_element_type=F32)


def _sigmoid(x):
    return 1.0 / (1.0 + jnp.exp(-x))


def _modulated_norm(x, g, mod):
    ms = jnp.mean(x * x, axis=-1, keepdims=True)
    return (x * lax.rsqrt(ms + EPS)) * (g * (1.0 + mod[1:2])) + mod[0:1]


def _ada_kernel(c_ref, w_ref, b_ref, o_ref):
    c = c_ref[...]
    s = (c * _sigmoid(c)).astype(BF16)
    o_ref[...] = jnp.dot(s, w_ref[...], preferred_element_type=F32) + b_ref[...]


def _ada_mod(c, w_ada_bf, b_ada):
    nb = c.shape[0]
    return pl.pallas_call(
        _ada_kernel,
        out_shape=jax.ShapeDtypeStruct((nb, 3 * D_MODEL), F32),
        compiler_params=pltpu.CompilerParams(vmem_limit_bytes=VMEM_LIMIT),
        name="ada_mod",
    )(c, w_ada_bf, b_ada)


def _norm_rope_T(rows, n_heads, c_tab, s_tab, perm_slices):
    ts = rows.shape[-1]
    v = rows.reshape(n_heads, HEAD_DIM, ts)
    ms = jnp.mean(v * v, axis=1, keepdims=True)
    vh = v * lax.rsqrt(ms + EPS)
    partner = jnp.concatenate([vh[:, a:b] for a, b in perm_slices], axis=1)
    return vh * c_tab[None] + partner * s_tab[None]


_PERM_A = ((16, 32), (0, 16), (48, 64), (32, 48))
_PERM_B = ((8, 16), (0, 8), (16, 64))


def _inproj_kernel(x_ref, mod_ref, g_ref, wT_ref, tab_ref,
                   qaT_ref, vaT_ref, qbT_ref, vbT_ref, ka_ref, kb_ref):
    g = g_ref[...]
    mod = mod_ref[0]

    def finish(group, tok, pT):
        lanes = slice(tok, tok + IN_SUB)
        if group == "qa":
            r = _norm_rope_T(pT, HQ_A, tab_ref[0, :, lanes], tab_ref[1, :, lanes], _PERM_A)
            qaT_ref[0, :, :, lanes] = r.astype(BF16)
        elif group == "qb":
            r = _norm_rope_T(pT, 2 * H_B, tab_ref[4, :, lanes], tab_ref[5, :, lanes], _PERM_B)
            qbT_ref[0, :, :, lanes] = r.astype(BF16)
        elif group == "k":
            rb = _norm_rope_T(pT[:B_QK], 2 * H_B, tab_ref[6, :, lanes], tab_ref[7, :, lanes], _PERM_B)
            for hh in range(2 * H_B):
                kb_ref[0, hh, lanes, :] = rb[hh].T.astype(BF16)
            ra = _norm_rope_T(pT[B_QK:], HKV_A, tab_ref[2, :, lanes], tab_ref[3, :, lanes], _PERM_A)
            for hh in range(HKV_A):
                ka_ref[0, hh, lanes, :] = ra[hh].T.astype(BF16)
        else:
            vbT_ref[0, :, 0, :, lanes] = pT[:B_V].reshape(H_B, 2 * HEAD_DIM, IN_SUB).astype(BF16)
            vaT_ref[0, :, 0, :, lanes] = pT[B_V:].reshape(HKV_A, HEAD_DIM, IN_SUB).astype(BF16)

    groups = (("qa", _R_QA, _R_QB), ("qb", _R_QB, _R_KB), ("k", _R_KB, _R_VB), ("v", _R_VB, _R_END))
    pending = []
    for tok in range(0, TS_IN, IN_SUB):
        h = _modulated_norm(x_ref[0, tok:tok + IN_SUB, :], g, mod).astype(BF16)
        for name, lo, hi in groups:
            pending.append((name, tok, _nt_dot(wT_ref[lo:hi, :], h)))
            if len(pending) > IN_LOOKAHEAD:
                finish(*pending.pop(0))
    while pending:
        finish(*pending.pop(0))


def _inproj(x, mod3, norm_g, wT_att, tables):
    nb, s, d = x.shape
    nt = s // TS_IN
    out_shape = (
        jax.ShapeDtypeStruct((nb, HQ_A, HEAD_DIM, s), BF16),
        jax.ShapeDtypeStruct((nb, HKV_A, nt, HEAD_DIM, TS_IN), BF16),
        jax.ShapeDtypeStruct((nb, 2 * H_B, HEAD_DIM, s), BF16),
        jax.ShapeDtypeStruct((nb, H_B, nt, 2 * HEAD_DIM, TS_IN), BF16),
        jax.ShapeDtypeStruct((nb, HKV_A, s, HEAD_DIM), BF16),
        jax.ShapeDtypeStruct((nb, 2 * H_B, s, HEAD_DIM), BF16),
    )
    const = pl.Buffered(1)
    in_specs = [
        pl.BlockSpec((1, TS_IN, d), lambda t, b: (b, t, 0)),
        pl.BlockSpec((1, 3, d), lambda t, b: (b, 0, 0)),
        pl.BlockSpec((1, d), lambda t, b: (0, 0), pipeline_mode=const),
        pl.BlockSpec((_R_END, d), lambda t, b: (0, 0), pipeline_mode=const),
        pl.BlockSpec((8, HEAD_DIM, TS_IN), lambda t, b: (0, 0, t)),
    ]
    out_specs = (
        pl.BlockSpec((1, HQ_A, HEAD_DIM, TS_IN), lambda t, b: (b, 0, 0, t)),
        pl.BlockSpec((1, HKV_A, 1, HEAD_DIM, TS_IN), lambda t, b: (b, 0, t, 0, 0)),
        pl.BlockSpec((1, 2 * H_B, HEAD_DIM, TS_IN), lambda t, b: (b, 0, 0, t)),
        pl.BlockSpec((1, H_B, 1, 2 * HEAD_DIM, TS_IN), lambda t, b: (b, 0, t, 0, 0)),
        pl.BlockSpec((1, HKV_A, TS_IN, HEAD_DIM), lambda t, b: (b, 0, t, 0)),
        pl.BlockSpec((1, 2 * H_B, TS_IN, HEAD_DIM), lambda t, b: (b, 0, t, 0)),
    )
    return pl.pallas_call(
        _inproj_kernel,
        grid=(nt, nb),
        in_specs=in_specs,
        out_specs=out_specs,
        out_shape=out_shape,
        compiler_params=pltpu.CompilerParams(
            dimension_semantics=("arbitrary", "arbitrary"), vmem_limit_bytes=VMEM_LIMIT),
        name="in_proj",
    )(x, mod3, norm_g, wT_att, tables)


def _attend_tile(n_maps, q_of, k_of, v_of, acc_ref, m_ref, l_ref, *, online):
    tq = acc_ref.shape[-1]
    s_len = k_of(0).shape[0]
    acc_ref[...] = jnp.zeros(acc_ref.shape, F32)
    l_ref[...] = jnp.zeros(l_ref.shape, F32)
    if online:
        m_ref[...] = jnp.full(m_ref.shape, NEG_BIG, F32)
    subs = TS_IN // KV_SUB

    def scores(i, c, w):
        off = pl.multiple_of(c * TS_IN + w * KV_SUB, KV_SUB)
        return jnp.dot(k_of(i)[pl.ds(off, KV_SUB), :], q_of(i), preferred_element_type=F32)

    def online_body(j, carry):
        for u in range(KV_CHUNKS_PER_ITER):
            c = j * KV_CHUNKS_PER_ITER + u
            for w in range(subs):
                for i in range(n_maps):
                    s = scores(i, c, w)
                    m_prev = m_ref[i]
                    m_new = jnp.maximum(m_prev, jnp.max(s, axis=0, keepdims=True))
                    alpha = jnp.exp2(m_prev - m_new)
                    p = jnp.exp2(s - m_new)
                    l_ref[i] = alpha * l_ref[i] + p.reshape(KV_SUB // 8, 8, tq).sum(axis=0)
                    pv = jnp.dot(v_of(c, w), p.astype(BF16), preferred_element_type=F32)
                    acc_ref[i] = alpha * acc_ref[i] + pv
                    m_ref[i] = m_new
        return carry

    def raw_body(j, carry):
        items = [(i, u, w) for u in range(KV_CHUNKS_PER_ITER) for w in range(subs) for i in range(n_maps)]
        l_part = [None] * n_maps
        pv_part = [None] * n_maps
        pending = []

        def consume(item, s):
            i, u, w = item
            p = jnp.exp2(s)
            lp = p.reshape(KV_SUB // 8, 8, tq).sum(axis=0)
            pv = jnp.dot(v_of(j * KV_CHUNKS_PER_ITER + u, w), p.astype(BF16), preferred_element_type=F32)
            l_part[i] = lp if l_part[i] is None else l_part[i] + lp
            pv_part[i] = pv if pv_part[i] is None else pv_part[i] + pv

        for item in items:
            i, u, w = item
            pending.append((item, scores(i, j * KV_CHUNKS_PER_ITER + u, w)))
            if len(pending) > SCORE_LOOKAHEAD:
                consume(*pending.pop(0))
        while pending:
            consume(*pending.pop(0))
        for i in range(n_maps):
            l_ref[i] += l_part[i]
            acc_ref[i] += pv_part[i]
        return carry

    lax.fori_loop(0, s_len // (TS_IN * KV_CHUNKS_PER_ITER), online_body if online else raw_body, 0)


def _attn_a_kernel(qT_ref, k_ref, vT_ref, o_ref, acc_ref, m_ref, l_ref, *, online):
    _attend_tile(GROUP_A, lambda i: qT_ref[0, i], lambda i: k_ref.at[0, 0],
                 lambda c, w: vT_ref[0, 0, c, :, w * KV_SUB:(w + 1) * KV_SUB],
                 acc_ref, m_ref, l_ref, online=online)
    l = jnp.sum(l_ref[...], axis=1, keepdims=True)
    oT = acc_ref[...] * (1.0 / l)
    o_ref[0] = oT.reshape(GROUP_A * HEAD_DIM, oT.shape[-1]).T


def _attn_b_kernel(lam_ref, qT_ref, k_ref, vT_ref, o_ref, acc_ref, m_ref, l_ref, *, online):
    lv = lam_ref[...]
    lam = (jnp.exp(jnp.sum(lv[0:1] * lv[1:2], axis=1, keepdims=True))
           - jnp.exp(jnp.sum(lv[2:3] * lv[3:4], axis=1, keepdims=True)) + LAM_INIT)
    _attend_tile(2, lambda i: qT_ref[0, i], lambda i: k_ref.at[0, i],
                 lambda c, w: vT_ref[0, 0, c, :, w * KV_SUB:(w + 1) * KV_SUB],
                 acc_ref, m_ref, l_ref, online=online)
    l = jnp.sum(l_ref[...], axis=1, keepdims=True)
    o = acc_ref[...] * (1.0 / l)
    o_ref[0] = (o[0] - lam * o[1]).T


def _attn_a(qaT, ka, vaT, *, online):
    nb, _, _, s = qaT.shape
    nc = vaT.shape[2]
    width = GROUP_A * HEAD_DIM
    return pl.pallas_call(
        functools.partial(_attn_a_kernel, online=online),
        grid=(nb, HKV_A, s // TQ),
        in_specs=[
            pl.BlockSpec((1, GROUP_A, HEAD_DIM, TQ), lambda b, g, i: (b, g, 0, i)),
            pl.BlockSpec((1, 1, s, HEAD_DIM), lambda b, g, i: (b, g, 0, 0)),
            pl.BlockSpec((1, 1, nc, HEAD_DIM, TS_IN), lambda b, g, i: (b, g, 0, 0, 0)),
        ],
        out_specs=pl.BlockSpec((1, TQ, width), lambda b, g, i: (b, i, g)),
        out_shape=jax.ShapeDtypeStruct((nb, s, A_Q), F32),
        scratch_shapes=[
            pltpu.VMEM((GROUP_A, HEAD_DIM, TQ), F32),
            pltpu.VMEM((GROUP_A, 1, TQ), F32),
            pltpu.VMEM((GROUP_A, 8, TQ), F32),
        ],
        compiler_params=pltpu.CompilerParams(
            dimension_semantics=("arbitrary", "arbitrary", "arbitrary"), vmem_limit_bytes=VMEM_LIMIT),
        name="attn_gqa_online" if online else "attn_gqa",
    )(qaT, ka, vaT)


def _attn_b(lam_vecs, qbT, kb, vbT, *, online):
    nb, _, _, s = qbT.shape
    nc = vbT.shape[2]
    dv = 2 * HEAD_DIM
    return pl.pallas_call(
        functools.partial(_attn_b_kernel, online=online),
        grid=(nb, H_B, s // TQ),
        in_specs=[
            pl.BlockSpec((4, HEAD_DIM), lambda b, h, i: (0, 0)),
            pl.BlockSpec((1, 2, HEAD_DIM, TQ), lambda b, h, i: (b, h, 0, i)),
            pl.BlockSpec((1, 2, s, HEAD_DIM), lambda b, h, i: (b, h, 0, 0)),
            pl.BlockSpec((1, 1, nc, dv, TS_IN), lambda b, h, i: (b, h, 0, 0, 0)),
        ],
        out_specs=pl.BlockSpec((1, TQ, dv), lambda b, h, i: (b, i, h)),
        out_shape=jax.ShapeDtypeStruct((nb, s, B_V), F32),
        scratch_shapes=[
            pltpu.VMEM((2, dv, TQ), F32),
            pltpu.VMEM((2, 1, TQ), F32),
            pltpu.VMEM((2, 8, TQ), F32),
        ],
        compiler_params=pltpu.CompilerParams(
            dimension_semantics=("arbitrary", "arbitrary", "arbitrary"), vmem_limit_bytes=VMEM_LIMIT),
        name="attn_diff_online" if online else "attn_diff",
    )(lam_vecs, qbT, kb, vbT)


def _out_kernel(x_ref, mod_ref, g_ref, oa_ref, ob_ref, wg_ref, wpa_ref, wpb_ref, wo_ref, sg_ref, y_ref):
    x = x_ref[0]
    mod = mod_ref[0]
    h = _modulated_norm(x, g_ref[...], mod).astype(BF16)
    gates = jnp.dot(h, wg_ref[...], preferred_element_type=F32)
    za = gates[:, 0:A_Q]
    zb = gates[:, A_Q:A_Q + B_V]
    ga = gates[:, A_Q + B_V:A_Q + B_V + D_MODEL]
    gb = gates[:, A_Q + B_V + D_MODEL:]
    a = oa_ref[0] * (za * _sigmoid(za))
    pa = jnp.dot(a.astype(BF16), wpa_ref[...], preferred_element_type=F32)
    dv = 2 * HEAD_DIM
    ob = ob_ref[0]
    sg = sg_ref[...] * (1.0 - LAM_INIT)
    normed = []
    for hh in range(H_B):
        oh = ob[:, hh * dv:(hh + 1) * dv]
        ms = jnp.mean(oh * oh, axis=-1, keepdims=True)
        normed.append(oh * lax.rsqrt(ms + EPS) * sg)
    bn = jnp.concatenate(normed, axis=-1) * (zb * _sigmoid(zb))
    pb = jnp.dot(bn.astype(BF16), wpb_ref[...], preferred_element_type=F32)
    merged = _sigmoid(ga) * pa + _sigmoid(gb) * pb
    out = jnp.dot(merged.astype(BF16), wo_ref[...], preferred_element_type=F32)
    y_ref[0] = x + mod[2:3] * out


def _out_stage(x, mod3, norm_g, oa, ob, w_gates, w_pa, w_pb, w_out, subln_g):
    nb, s, d = x.shape
    const = pl.Buffered(1)
    n_gate = w_gates.shape[1]
    return pl.pallas_call(
        _out_kernel,
        grid=(nb, s // TS_OUT),
        in_specs=[
            pl.BlockSpec((1, TS_OUT, d), lambda b, t: (b, t, 0)),
            pl.BlockSpec((1, 3, d), lambda b, t: (b, 0, 0)),
            pl.BlockSpec((1, d), lambda b, t: (0, 0), pipeline_mode=const),
            pl.BlockSpec((1, TS_OUT, A_Q), lambda b, t: (b, t, 0)),
            pl.BlockSpec((1, TS_OUT, B_V), lambda b, t: (b, t, 0)),
            pl.BlockSpec((d, n_gate), lambda b, t: (0, 0), pipeline_mode=const),
            pl.BlockSpec((A_Q, d), lambda b, t: (0, 0), pipeline_mode=const),
            pl.BlockSpec((B_V, d), lambda b, t: (0, 0), pipeline_mode=const),
            pl.BlockSpec((d, d), lambda b, t: (0, 0), pipeline_mode=const),
            pl.BlockSpec((1, 2 * HEAD_DIM), lambda b, t: (0, 0), pipeline_mode=const),
        ],
        out_specs=pl.BlockSpec((1, TS_OUT, d), lambda b, t: (b, t, 0)),
        out_shape=jax.ShapeDtypeStruct((nb, s, d), F32),
        compiler_params=pltpu.CompilerParams(
            dimension_semantics=("arbitrary", "arbitrary"), vmem_limit_bytes=VMEM_LIMIT),
        name="out_stage",
    )(x, mod3, norm_g, oa, ob, w_gates, w_pa, w_pb, w_out, subln_g)


def _rope_coeff_tables(qn_a, kn_a, qn_b, kn_b, s):
    t = jnp.arange(s)
    row = (t // GRID_W).astype(F32)
    col = (t % GRID_W).astype(F32)
    inv_ax = AX_THETA ** (-jnp.arange(0, AX_DIM, 2, dtype=F32) / AX_DIM)
    ang_r = inv_ax[:, None] * row[None, :]
    ang_c = inv_ax[:, None] * col[None, :]
    ang_a = jnp.concatenate([ang_r, ang_r, ang_c, ang_c], axis=0)
    cos_a, sin_a = jnp.cos(ang_a), jnp.sin(ang_a)
    half = AX_DIM // 2
    sign_a = jnp.concatenate([-jnp.ones(half), jnp.ones(half), -jnp.ones(half), jnp.ones(half)]).astype(F32)
    perm_a = jnp.concatenate([jnp.arange(a, b) for a, b in _PERM_A])

    inv_p = ROPE_THETA ** (-jnp.arange(0, ROT_DIM, 2, dtype=F32) / ROT_DIM)
    ang_p = inv_p[:, None] * t.astype(F32)[None, :]
    ang_p = jnp.concatenate([ang_p, ang_p], axis=0)
    rest = HEAD_DIM - ROT_DIM
    cos_b = jnp.concatenate([jnp.cos(ang_p), jnp.ones((rest, s), F32)], axis=0)
    sin_b = jnp.concatenate([jnp.sin(ang_p), jnp.zeros((rest, s), F32)], axis=0)
    hb = ROT_DIM // 2
    sign_b = jnp.concatenate([-jnp.ones(hb), jnp.ones(hb), jnp.zeros(rest)]).astype(F32)
    perm_b = jnp.concatenate([jnp.arange(a, b) for a, b in _PERM_B])

    def pair(g, cos, sin, sign, perm, scale):
        g = g.astype(F32)
        return [scale * g[:, None] * cos, scale * (sign * g[perm])[:, None] * sin]

    tabs = (pair(qn_a, cos_a, sin_a, sign_a, perm_a, Q_SCALE) + pair(kn_a, cos_a, sin_a, sign_a, perm_a, 1.0)
            + pair(qn_b, cos_b, sin_b, sign_b, perm_b, Q_SCALE) + pair(kn_b, cos_b, sin_b, sign_b, perm_b, 1.0))
    return jnp.stack(tabs, axis=0)


def _layer(x, c, wts, tables, safe):
    (w_ada_bf, b_ada, norm_g, wT_att, w_gates, w_pa, w_pb, w_out, subln_g, lam_vecs) = wts
    nb = x.shape[0]
    mod3 = _ada_mod(c, w_ada_bf, b_ada).reshape(nb, 3, D_MODEL)
    qaT, vaT, qbT, vbT, ka, kb = _inproj(x, mod3, norm_g, wT_att, tables)

    def attend(online):
        def run(ops):
            qaT_, ka_, vaT_, qbT_, kb_, vbT_ = ops
            return (_attn_a(qaT_, ka_, vaT_, online=online),
                    _attn_b(lam_vecs, qbT_, kb_, vbT_, online=online))
        return run

    oa, ob = lax.cond(safe, attend(False), attend(True), (qaT, ka, vaT, qbT, kb, vbT))
    return _out_stage(x, mod3, norm_g, oa, ob, w_gates, w_pa, w_pb, w_out, subln_g)


def kernel(x_prompt, x_sample, c_prompt, c_sample, w_ada, b_ada, norm_g, w_in, qn_a, kn_a, qn_b, kn_b,
           lam_q1, lam_k1, lam_q2, lam_k2, subln_g, w_proj_a, w_proj_b, w_out):
    w = w_in[0]
    cols = lambda a, b: w[:, a:b]
    wT_att = jnp.concatenate(
        [cols(_O_QA, _O_KA), cols(_O_QB, _O_KB), cols(_O_KB, _O_VB), cols(_O_KA, _O_VA),
         cols(_O_VB, _O_ZB), cols(_O_VA, _O_ZA)], axis=1).T.astype(BF16)
    w_gates = jnp.concatenate(
        [cols(_O_ZA, _O_QB), cols(_O_ZB, _O_GA), cols(_O_GA, _O_END)], axis=1).astype(BF16)
    wts = (w_ada[0].astype(BF16), b_ada[0][None, :], norm_g[0][None, :], wT_att, w_gates,
           w_proj_a[0].astype(BF16), w_proj_b[0].astype(BF16), w_out[0].astype(BF16),
           subln_g[0][None, :], jnp.stack([lam_q1[0], lam_k1[0], lam_q2[0], lam_k2[0]], axis=0).astype(F32))
    tables = _rope_coeff_tables(qn_a[0], kn_a[0], qn_b[0], kn_b[0], SEQ)
    bound_a = math.sqrt(HEAD_DIM) * jnp.max(jnp.abs(qn_a[0])) * jnp.max(jnp.abs(kn_a[0]))
    bound_b = math.sqrt(HEAD_DIM) * jnp.max(jnp.abs(qn_b[0])) * jnp.max(jnp.abs(kn_b[0]))
    safe = jnp.maximum(bound_a, bound_b) <= SAFE_SCORE_BOUND
    y_prompt = _layer(x_prompt, c_prompt, wts, tables, safe)
    y_sample = _layer(x_sample, c_sample, wts, tables, safe)
    return (y_prompt, y_sample)
```

```python
import functools
import math

import jax
import jax.numpy as jnp
from jax import lax
from jax.experimental import pallas as pl
from jax.experimental.pallas import tpu as pltpu

F32 = jnp.float32
BF16 = jnp.bfloat16

D_MODEL = 1024
SEQ = 8192
GRID_W = 64
HEAD_DIM = 64
HQ_A = 8
HKV_A = 2
GROUP_A = HQ_A // HKV_A
AX_DIM = HEAD_DIM // 2
AX_THETA = 10000.0
A_Q = HQ_A * HEAD_DIM
A_KV = HKV_A * HEAD_DIM
H_B = 4
B_QK = H_B * 2 * HEAD_DIM
B_V = H_B * 2 * HEAD_DIM
ROT_DIM = HEAD_DIM // 4
ROPE_THETA = 500000.0
EPS = 1e-6
LOG2E = math.log2(math.e)
Q_SCALE = LOG2E / math.sqrt(HEAD_DIM)
LAM_INIT = 0.8 - 0.6 * math.exp(-0.3 * 0)

_O_QA = 0
_O_KA = _O_QA + A_Q
_O_VA = _O_KA + A_KV
_O_ZA = _O_VA + A_KV
_O_QB = _O_ZA + A_Q
_O_KB = _O_QB + B_QK
_O_VB = _O_KB + B_QK
_O_ZB = _O_VB + B_V
_O_GA = _O_ZB + B_V
_O_GB = _O_GA + D_MODEL
_O_END = _O_GB + D_MODEL

_R_QA = 0
_R_QB = _R_QA + A_Q
_R_KB = _R_QB + B_QK
_R_KA = _R_KB + B_QK
_R_VB = _R_KA + A_KV
_R_VA = _R_VB + B_V
_R_END = _R_VA + A_KV

TS_IN = 512
IN_SUB = 256
IN_LOOKAHEAD = 2
TS_OUT = 512
TQ = 512
Q_TILES_PER_STEP = 16
KV_SUB = 256
SCORE_LOOKAHEAD = 3
VMEM_LIMIT = 56 * 1024 * 1024
SAFE_SCORE_BOUND = 40.0
NEG_BIG = -1e30
assert TQ == TS_IN and TS_IN % IN_SUB == 0 and TS_IN % KV_SUB == 0


def _nt_dot(a, b):
    return lax.dot_general(a, b, (((1,), (1,)), ((), ())), preferred_element_type=F32)


def _sigmoid(x):
    return 1.0 / (1.0 + jnp.exp(-x))


def _modulated_norm(x, g, mod):
    ms = jnp.mean(x * x, axis=-1, keepdims=True)
    return (x * lax.rsqrt(ms + EPS)) * (g * (1.0 + mod[1:2])) + mod[0:1]


def _ada_kernel(c_ref, w_ref, b_ref, o_ref):
    c = c_ref[...]
    s = (c * _sigmoid(c)).astype(BF16)
    o_ref[...] = jnp.dot(s, w_ref[...], preferred_element_type=F32) + b_ref[...]


def _ada_mod(c, w_ada_bf, b_ada):
    nb = c.shape[0]
    return pl.pallas_call(
        _ada_kernel,
        out_shape=jax.ShapeDtypeStruct((nb, 3 * D_MODEL), F32),
        compiler_params=pltpu.CompilerParams(vmem_limit_bytes=VMEM_LIMIT),
        name="ada_mod",
    )(c, w_ada_bf, b_ada)


def _norm_rope_T(rows, n_heads, c_tab, s_tab, perm_slices):
    ts = rows.shape[-1]
    v = rows.reshape(n_heads, HEAD_DIM, ts)
    ms = jnp.mean(v * v, axis=1, keepdims=True)
    vh = v * lax.rsqrt(ms + EPS)
    partner = jnp.concatenate([vh[:, a:b] for a, b in perm_slices], axis=1)
    return vh * c_tab[None] + partner * s_tab[None]


_PERM_A = ((16, 32), (0, 16), (48, 64), (32, 48))
_PERM_B = ((8, 16), (0, 8), (16, 64))


def _inproj_kernel(x_ref, mod_ref, g_ref, wT_ref, tab_ref,
                   qaT_ref, vaT_ref, qbT_ref, vbT_ref, ka_ref, kb_ref):
    g = g_ref[...]
    mod = mod_ref[0]

    def finish(group, tok, pT):
        lanes = slice(tok, tok + IN_SUB)
        if group == "qa":
            r = _norm_rope_T(pT, HQ_A, tab_ref[0, :, lanes], tab_ref[1, :, lanes], _PERM_A)
            qaT_ref[0, :, 0, :, lanes] = r.astype(BF16)
        elif group == "qb":
            r = _norm_rope_T(pT, 2 * H_B, tab_ref[4, :, lanes], tab_ref[5, :, lanes], _PERM_B)
            qbT_ref[0, :, 0, :, lanes] = r.astype(BF16)
        elif group == "k":
            rb = _norm_rope_T(pT[:B_QK], 2 * H_B, tab_ref[6, :, lanes], tab_ref[7, :, lanes], _PERM_B)
            for hh in range(2 * H_B):
                kb_ref[0, hh, lanes, :] = rb[hh].T.astype(BF16)
            ra = _norm_rope_T(pT[B_QK:], HKV_A, tab_ref[2, :, lanes], tab_ref[3, :, lanes], _PERM_A)
            for hh in range(HKV_A):
                ka_ref[0, hh, lanes, :] = ra[hh].T.astype(BF16)
        else:
            vbT_ref[0, :, 0, :, lanes] = pT[:B_V].reshape(H_B, 2 * HEAD_DIM, IN_SUB).astype(BF16)
            vaT_ref[0, :, 0, :, lanes] = pT[B_V:].reshape(HKV_A, HEAD_DIM, IN_SUB).astype(BF16)

    groups = (("qa", _R_QA, _R_QB), ("qb", _R_QB, _R_KB), ("k", _R_KB, _R_VB), ("v", _R_VB, _R_END))
    pending = []
    for tok in range(0, TS_IN, IN_SUB):
        h = _modulated_norm(x_ref[0, tok:tok + IN_SUB, :], g, mod).astype(BF16)
        for name, lo, hi in groups:
            pending.append((name, tok, _nt_dot(wT_ref[lo:hi, :], h)))
            if len(pending) > IN_LOOKAHEAD:
                finish(*pending.pop(0))
    while pending:
        finish(*pending.pop(0))


def _inproj(x, mod3, norm_g, wT_att, tables):
    nb, s, d = x.shape
    nt = s // TS_IN
    out_shape = (
        jax.ShapeDtypeStruct((nb, HQ_A, nt, HEAD_DIM, TS_IN), BF16),
        jax.ShapeDtypeStruct((nb, HKV_A, nt, HEAD_DIM, TS_IN), BF16),
        jax.ShapeDtypeStruct((nb, 2 * H_B, nt, HEAD_DIM, TS_IN), BF16),
        jax.ShapeDtypeStruct((nb, H_B, nt, 2 * HEAD_DIM, TS_IN), BF16),
        jax.ShapeDtypeStruct((nb, HKV_A, s, HEAD_DIM), BF16),
        jax.ShapeDtypeStruct((nb, 2 * H_B, s, HEAD_DIM), BF16),
    )
    const = pl.Buffered(1)
    in_specs = [
        pl.BlockSpec((1, TS_IN, d), lambda t, b: (b, t, 0)),
        pl.BlockSpec((1, 3, d), lambda t, b: (b, 0, 0)),
        pl.BlockSpec((1, d), lambda t, b: (0, 0), pipeline_mode=const),
        pl.BlockSpec((_R_END, d), lambda t, b: (0, 0), pipeline_mode=const),
        pl.BlockSpec((8, HEAD_DIM, TS_IN), lambda t, b: (0, 0, t)),
    ]
    out_specs = (
        pl.BlockSpec((1, HQ_A, 1, HEAD_DIM, TS_IN), lambda t, b: (b, 0, t, 0, 0)),
        pl.BlockSpec((1, HKV_A, 1, HEAD_DIM, TS_IN), lambda t, b: (b, 0, t, 0, 0)),
        pl.BlockSpec((1, 2 * H_B, 1, HEAD_DIM, TS_IN), lambda t, b: (b, 0, t, 0, 0)),
        pl.BlockSpec((1, H_B, 1, 2 * HEAD_DIM, TS_IN), lambda t, b: (b, 0, t, 0, 0)),
        pl.BlockSpec((1, HKV_A, TS_IN, HEAD_DIM), lambda t, b: (b, 0, t, 0)),
        pl.BlockSpec((1, 2 * H_B, TS_IN, HEAD_DIM), lambda t, b: (b, 0, t, 0)),
    )
    return pl.pallas_call(
        _inproj_kernel,
        grid=(nt, nb),
        in_specs=in_specs,
        out_specs=out_specs,
        out_shape=out_shape,
        compiler_params=pltpu.CompilerParams(
            dimension_semantics=("arbitrary", "arbitrary"), vmem_limit_bytes=VMEM_LIMIT),
        name="in_proj",
    )(x, mod3, norm_g, wT_att, tables)


def _attend_tile(n_maps, q_of, k_of, v_of, acc_ref, m_ref, l_ref, *, online):
    tq = acc_ref.shape[-1]
    n_chunks = k_of(0).shape[0] // TS_IN
    subs = TS_IN // KV_SUB

    def scores(i, c, w):
        off = c * TS_IN + w * KV_SUB
        if not isinstance(off, int):
            off = pl.multiple_of(off, KV_SUB)
        return jnp.dot(k_of(i)[pl.ds(off, KV_SUB), :], q_of(i), preferred_element_type=F32)

    def normalised(acc, l8):
        return acc * (1.0 / jnp.sum(l8, axis=0, keepdims=True))

    if online:
        acc_ref[...] = jnp.zeros(acc_ref.shape, F32)
        l_ref[...] = jnp.zeros(l_ref.shape, F32)
        m_ref[...] = jnp.full(m_ref.shape, NEG_BIG, F32)

        def online_body(c, carry):
            for w in range(subs):
                for i in range(n_maps):
                    s = scores(i, c, w)
                    m_prev = m_ref[i]
                    m_new = jnp.maximum(m_prev, jnp.max(s, axis=0, keepdims=True))
                    alpha = jnp.exp2(m_prev - m_new)
                    p = jnp.exp2(s - m_new)
                    l_ref[i] = alpha * l_ref[i] + p.reshape(KV_SUB // 8, 8, tq).sum(axis=0)
                    pv = jnp.dot(v_of(c, w), p.astype(BF16), preferred_element_type=F32)
                    acc_ref[i] = alpha * acc_ref[i] + pv
                    m_ref[i] = m_new
            return carry

        lax.fori_loop(0, n_chunks, online_body, 0)
        return [normalised(acc_ref[i], l_ref[i]) for i in range(n_maps)]

    items = [(i, c, w) for c in range(n_chunks) for w in range(subs) for i in range(n_maps)]
    l_part = [None] * n_maps
    pv_part = [None] * n_maps
    pending = []

    def consume(item, s):
        i, c, w = item
        p = jnp.exp2(s)
        lp = p.reshape(KV_SUB // 8, 8, tq).sum(axis=0)
        pv = jnp.dot(v_of(c, w), p.astype(BF16), preferred_element_type=F32)
        l_part[i] = lp if l_part[i] is None else l_part[i] + lp
        pv_part[i] = pv if pv_part[i] is None else pv_part[i] + pv

    for item in items:
        pending.append((item, scores(*item)))
        if len(pending) > SCORE_LOOKAHEAD:
            consume(*pending.pop(0))
    while pending:
        consume(*pending.pop(0))
    return [normalised(pv_part[i], l_part[i]) for i in range(n_maps)]


def _attn_a_kernel(qT_ref, k_ref, vT_ref, o_ref, acc_ref, m_ref, l_ref, *, online):
    def q_tile(t, carry):
        o = _attend_tile(GROUP_A, lambda i: qT_ref[0, i, t], lambda i: k_ref.at[0, 0],
                         lambda c, w: vT_ref[0, 0, c, :, w * KV_SUB:(w + 1) * KV_SUB],
                         acc_ref, m_ref, l_ref, online=online)
        rows = pl.ds(pl.multiple_of(t * TQ, TQ), TQ)
        o_ref[0, rows, :] = jnp.concatenate(o, axis=0).T
        return carry

    lax.fori_loop(0, qT_ref.shape[2], q_tile, 0)


def _attn_b_kernel(lam_ref, qT_ref, k_ref, vT_ref, o_ref, acc_ref, m_ref, l_ref, *, online):
    lv = lam_ref[...]
    lam = (jnp.exp(jnp.sum(lv[0:1] * lv[1:2], axis=1, keepdims=True))
           - jnp.exp(jnp.sum(lv[2:3] * lv[3:4], axis=1, keepdims=True)) + LAM_INIT)

    def q_tile(t, carry):
        o = _attend_tile(2, lambda i: qT_ref[0, i, t], lambda i: k_ref.at[0, i],
                         lambda c, w: vT_ref[0, 0, c, :, w * KV_SUB:(w + 1) * KV_SUB],
                         acc_ref, m_ref, l_ref, online=online)
        rows = pl.ds(pl.multiple_of(t * TQ, TQ), TQ)
        o_ref[0, rows, :] = (o[0] - lam * o[1]).T
        return carry

    lax.fori_loop(0, qT_ref.shape[2], q_tile, 0)


def _attn_a(qaT, ka, vaT, *, online):
    nb, _, nq, _, _ = qaT.shape
    s = nq * TQ
    nc = vaT.shape[2]
    width = GROUP_A * HEAD_DIM
    return pl.pallas_call(
        functools.partial(_attn_a_kernel, online=online),
        grid=(nb, HKV_A, nq // Q_TILES_PER_STEP),
        in_specs=[
            pl.BlockSpec((1, GROUP_A, Q_TILES_PER_STEP, HEAD_DIM, TQ), lambda b, g, i: (b, g, i, 0, 0)),
            pl.BlockSpec((1, 1, s, HEAD_DIM), lambda b, g, i: (b, g, 0, 0)),
            pl.BlockSpec((1, 1, nc, HEAD_DIM, TS_IN), lambda b, g, i: (b, g, 0, 0, 0)),
        ],
        out_specs=pl.BlockSpec((1, Q_TILES_PER_STEP * TQ, width), lambda b, g, i: (b, i, g)),
        out_shape=jax.ShapeDtypeStruct((nb, s, A_Q), F32),
        scratch_shapes=[
            pltpu.VMEM((GROUP_A, HEAD_DIM, TQ), F32),
            pltpu.VMEM((GROUP_A, 1, TQ), F32),
            pltpu.VMEM((GROUP_A, 8, TQ), F32),
        ],
        compiler_params=pltpu.CompilerParams(
            dimension_semantics=("arbitrary", "arbitrary", "arbitrary"), vmem_limit_bytes=VMEM_LIMIT),
        name="attn_gqa_online" if online else "attn_gqa",
    )(qaT, ka, vaT)


def _attn_b(lam_vecs, qbT, kb, vbT, *, online):
    nb, _, nq, _, _ = qbT.shape
    s = nq * TQ
    nc = vbT.shape[2]
    dv = 2 * HEAD_DIM
    return pl.pallas_call(
        functools.partial(_attn_b_kernel, online=online),
        grid=(nb, H_B, nq // Q_TILES_PER_STEP),
        in_specs=[
            pl.BlockSpec((4, HEAD_DIM), lambda b, h, i: (0, 0)),
            pl.BlockSpec((1, 2, Q_TILES_PER_STEP, HEAD_DIM, TQ), lambda b, h, i: (b, h, i, 0, 0)),
            pl.BlockSpec((1, 2, s, HEAD_DIM), lambda b, h, i: (b, h, 0, 0)),
            pl.BlockSpec((1, 1, nc, dv, TS_IN), lambda b, h, i: (b, h, 0, 0, 0)),
        ],
        out_specs=pl.BlockSpec((1, Q_TILES_PER_STEP * TQ, dv), lambda b, h, i: (b, i, h)),
        out_shape=jax.ShapeDtypeStruct((nb, s, B_V), F32),
        scratch_shapes=[
            pltpu.VMEM((2, dv, TQ), F32),
            pltpu.VMEM((2, 1, TQ), F32),
            pltpu.VMEM((2, 8, TQ), F32),
        ],
        compiler_params=pltpu.CompilerParams(
            dimension_semantics=("arbitrary", "arbitrary", "arbitrary"), vmem_limit_bytes=VMEM_LIMIT),
        name="attn_diff_online" if online else "attn_diff",
    )(lam_vecs, qbT, kb, vbT)


def _out_kernel(x_ref, mod_ref, g_ref, oa_ref, ob_ref, wg_ref, wpa_ref, wpb_ref, wo_ref, sg_ref, y_ref):
    x = x_ref[0]
    mod = mod_ref[0]
    h = _modulated_norm(x, g_ref[...], mod).astype(BF16)
    gates = jnp.dot(h, wg_ref[...], preferred_element_type=F32)
    za = gates[:, 0:A_Q]
    zb = gates[:, A_Q:A_Q + B_V]
    ga = gates[:, A_Q + B_V:A_Q + B_V + D_MODEL]
    gb = gates[:, A_Q + B_V + D_MODEL:]
    a = oa_ref[0] * (za * _sigmoid(za))
    pa = jnp.dot(a.astype(BF16), wpa_ref[...], preferred_element_type=F32)
    dv = 2 * HEAD_DIM
    ob = ob_ref[0]
    sg = sg_ref[...] * (1.0 - LAM_INIT)
    normed = []
    for hh in range(H_B):
        oh = ob[:, hh * dv:(hh + 1) * dv]
        ms = jnp.mean(oh * oh, axis=-1, keepdims=True)
        normed.append(oh * lax.rsqrt(ms + EPS) * sg)
    bn = jnp.concatenate(normed, axis=-1) * (zb * _sigmoid(zb))
    pb = jnp.dot(bn.astype(BF16), wpb_ref[...], preferred_element_type=F32)
    merged = _sigmoid(ga) * pa + _sigmoid(gb) * pb
    out = jnp.dot(merged.astype(BF16), wo_ref[...], preferred_element_type=F32)
    y_ref[0] = x + mod[2:3] * out


def _out_stage(x, mod3, norm_g, oa, ob, w_gates, w_pa, w_pb, w_out, subln_g):
    nb, s, d = x.shape
    const = pl.Buffered(1)
    n_gate = w_gates.shape[1]
    return pl.pallas_call(
        _out_kernel,
        grid=(nb, s // TS_OUT),
        in_specs=[
            pl.BlockSpec((1, TS_OUT, d), lambda b, t: (b, t, 0)),
            pl.BlockSpec((1, 3, d), lambda b, t: (b, 0, 0)),
            pl.BlockSpec((1, d), lambda b, t: (0, 0), pipeline_mode=const),
            pl.BlockSpec((1, TS_OUT, A_Q), lambda b, t: (b, t, 0)),
            pl.BlockSpec((1, TS_OUT, B_V), lambda b, t: (b, t, 0)),
            pl.BlockSpec((d, n_gate), lambda b, t: (0, 0), pipeline_mode=const),
            pl.BlockSpec((A_Q, d), lambda b, t: (0, 0), pipeline_mode=const),
            pl.BlockSpec((B_V, d), lambda b, t: (0, 0), pipeline_mode=const),
            pl.BlockSpec((d, d), lambda b, t: (0, 0), pipeline_mode=const),
            pl.BlockSpec((1, 2 * HEAD_DIM), lambda b, t: (0, 0), pipeline_mode=const),
        ],
        out_specs=pl.BlockSpec((1, TS_OUT, d), lambda b, t: (b, t, 0)),
        out_shape=jax.ShapeDtypeStruct((nb, s, d), F32),
        compiler_params=pltpu.CompilerParams(
            dimension_semantics=("arbitrary", "arbitrary"), vmem_limit_bytes=VMEM_LIMIT),
        name="out_stage",
    )(x, mod3, norm_g, oa, ob, w_gates, w_pa, w_pb, w_out, subln_g)


def _rope_coeff_tables(qn_a, kn_a, qn_b, kn_b, s):
    t = jnp.arange(s)
    row = (t // GRID_W).astype(F32)
    col = (t % GRID_W).astype(F32)
    inv_ax = AX_THETA ** (-jnp.arange(0, AX_DIM, 2, dtype=F32) / AX_DIM)
    ang_r = inv_ax[:, None] * row[None, :]
    ang_c = inv_ax[:, None] * col[None, :]
    ang_a = jnp.concatenate([ang_r, ang_r, ang_c, ang_c], axis=0)
    cos_a, sin_a = jnp.cos(ang_a), jnp.sin(ang_a)
    half = AX_DIM // 2
    sign_a = jnp.concatenate([-jnp.ones(half), jnp.ones(half), -jnp.ones(half), jnp.ones(half)]).astype(F32)
    perm_a = jnp.concatenate([jnp.arange(a, b) for a, b in _PERM_A])

    inv_p = ROPE_THETA ** (-jnp.arange(0, ROT_DIM, 2, dtype=F32) / ROT_DIM)
    ang_p = inv_p[:, None] * t.astype(F32)[None, :]
    ang_p = jnp.concatenate([ang_p, ang_p], axis=0)
    rest = HEAD_DIM - ROT_DIM
    cos_b = jnp.concatenate([jnp.cos(ang_p), jnp.ones((rest, s), F32)], axis=0)
    sin_b = jnp.concatenate([jnp.sin(ang_p), jnp.zeros((rest, s), F32)], axis=0)
    hb = ROT_DIM // 2
    sign_b = jnp.concatenate([-jnp.ones(hb), jnp.ones(hb), jnp.zeros(rest)]).astype(F32)
    perm_b = jnp.concatenate([jnp.arange(a, b) for a, b in _PERM_B])

    def pair(g, cos, sin, sign, perm, scale):
        g = g.astype(F32)
        return [scale * g[:, None] * cos, scale * (sign * g[perm])[:, None] * sin]

    tabs = (pair(qn_a, cos_a, sin_a, sign_a, perm_a, Q_SCALE) + pair(kn_a, cos_a, sin_a, sign_a, perm_a, 1.0)
            + pair(qn_b, cos_b, sin_b, sign_b, perm_b, Q_SCALE) + pair(kn_b, cos_b, sin_b, sign_b, perm_b, 1.0))
    return jnp.stack(tabs, axis=0)


def _layer(x, c, wts, tables, safe):
    (w_ada_bf, b_ada, norm_g, wT_att, w_gates, w_pa, w_pb, w_out, subln_g, lam_vecs) = wts
    nb = x.shape[0]
    mod3 = _ada_mod(c, w_ada_bf, b_ada).reshape(nb, 3, D_MODEL)
    qaT, vaT, qbT, vbT, ka, kb = _inproj(x, mod3, norm_g, wT_att, tables)

    def attend(online):
        def run(ops):
            qaT_, ka_, vaT_, qbT_, kb_, vbT_ = ops
            return (_attn_a(qaT_, ka_, vaT_, online=online),
                    _attn_b(lam_vecs, qbT_, kb_, vbT_, online=online))
        return run

    oa, ob = lax.cond(safe, attend(False), attend(True), (qaT, ka, vaT, qbT, kb, vbT))
    return _out_stage(x, mod3, norm_g, oa, ob, w_gates, w_pa, w_pb, w_out, subln_g)


def kernel(x_prompt, x_sample, c_prompt, c_sample, w_ada, b_ada, norm_g, w_in, qn_a, kn_a, qn_b, kn_b,
           lam_q1, lam_k1, lam_q2, lam_k2, subln_g, w_proj_a, w_proj_b, w_out):
    w = w_in[0]
    cols = lambda a, b: w[:, a:b]
    wT_att = jnp.concatenate(
        [cols(_O_QA, _O_KA), cols(_O_QB, _O_KB), cols(_O_KB, _O_VB), cols(_O_KA, _O_VA),
         cols(_O_VB, _O_ZB), cols(_O_VA, _O_ZA)], axis=1).T.astype(BF16)
    w_gates = jnp.concatenate(
        [cols(_O_ZA, _O_QB), cols(_O_ZB, _O_GA), cols(_O_GA, _O_END)], axis=1).astype(BF16)
    wts = (w_ada[0].astype(BF16), b_ada[0][None, :], norm_g[0][None, :], wT_att, w_gates,
           w_proj_a[0].astype(BF16), w_proj_b[0].astype(BF16), w_out[0].astype(BF16),
           subln_g[0][None, :], jnp.stack([lam_q1[0], lam_k1[0], lam_q2[0], lam_k2[0]], axis=0).astype(F32))
    tables = _rope_coeff_tables(qn_a[0], kn_a[0], qn_b[0], kn_b[0], SEQ)
    bound_a = math.sqrt(HEAD_DIM) * jnp.max(jnp.abs(qn_a[0])) * jnp.max(jnp.abs(kn_a[0]))
    bound_b = math.sqrt(HEAD_DIM) * jnp.max(jnp.abs(qn_b[0])) * jnp.max(jnp.abs(kn_b[0]))
    safe = jnp.maximum(bound_a, bound_b) <= SAFE_SCORE_BOUND
    y_prompt = _layer(x_prompt, c_prompt, wts, tables, safe)
    y_sample = _layer(x_sample, c_sample, wts, tables, safe)
    return (y_prompt, y_sample)
```

```python
import functools
import math

import jax
import jax.numpy as jnp
from jax import lax
from jax.experimental import pallas as pl
from jax.experimental.pallas import tpu as pltpu

F32 = jnp.float32
BF16 = jnp.bfloat16

D_MODEL = 1024
SEQ = 8192
GRID_W = 64
HEAD_DIM = 64
HQ_A = 8
HKV_A = 2
GROUP_A = HQ_A // HKV_A
AX_DIM = HEAD_DIM // 2
AX_THETA = 10000.0
A_Q = HQ_A * HEAD_DIM
A_KV = HKV_A * HEAD_DIM
H_B = 4
B_QK = H_B * 2 * HEAD_DIM
B_V = H_B * 2 * HEAD_DIM
ROT_DIM = HEAD_DIM // 4
ROPE_THETA = 500000.0
EPS = 1e-6
LOG2E = math.log2(math.e)
Q_SCALE = LOG2E / math.sqrt(HEAD_DIM)
LAM_INIT = 0.8 - 0.6 * math.exp(-0.3 * 0)

_O_QA = 0
_O_KA = _O_QA + A_Q
_O_VA = _O_KA + A_KV
_O_ZA = _O_VA + A_KV
_O_QB = _O_ZA + A_Q
_O_KB = _O_QB + B_QK
_O_VB = _O_KB + B_QK
_O_ZB = _O_VB + B_V
_O_GA = _O_ZB + B_V
_O_GB = _O_GA + D_MODEL
_O_END = _O_GB + D_MODEL

_R_QA = 0
_R_QB = _R_QA + A_Q
_R_KB = _R_QB + B_QK
_R_KA = _R_KB + B_QK
_R_VB = _R_KA + A_KV
_R_VA = _R_VB + B_V
_R_END = _R_VA + A_KV

TS_IN = 512
IN_SUB = 256
IN_LOOKAHEAD = 1
TS_OUT = 512
TQ = 512
Q_TILES_PER_STEP = 16
KV_SUB = 256
SCORE_LOOKAHEAD = 3
VMEM_LIMIT = 56 * 1024 * 1024
SAFE_SCORE_BOUND = 40.0
NEG_BIG = -1e30
assert TQ == TS_IN and TS_IN % IN_SUB == 0 and TS_IN % KV_SUB == 0


def _nt_dot(a, b):
    return lax.dot_general(a, b, (((1,), (1,)), ((), ())), preferred_element_type=F32)


def _sigmoid(x):
    return 1.0 / (1.0 + jnp.exp(-x))


def _modulated_norm(x, g, mod):
    ms = jnp.mean(x * x, axis=-1, keepdims=True)
    return (x * lax.rsqrt(ms + EPS)) * (g * (1.0 + mod[1:2])) + mod[0:1]


def _ada_kernel(c_ref, w_ref, b_ref, o_ref):
    c = c_ref[...]
    s = (c * _sigmoid(c)).astype(BF16)
    o_ref[...] = jnp.dot(s, w_ref[...], preferred_element_type=F32) + b_ref[...]


def _ada_mod(c, w_ada_bf, b_ada):
    nb = c.shape[0]
    return pl.pallas_call(
        _ada_kernel,
        out_shape=jax.ShapeDtypeStruct((nb, 3 * D_MODEL), F32),
        compiler_params=pltpu.CompilerParams(vmem_limit_bytes=VMEM_LIMIT),
        name="ada_mod",
    )(c, w_ada_bf, b_ada)


def _norm_rope_T(rows, n_heads, c_tab, s_tab, perm_slices):
    ts = rows.shape[-1]
    v = rows.reshape(n_heads, HEAD_DIM, ts)
    ms = jnp.mean(v * v, axis=1, keepdims=True)
    vh = v * lax.rsqrt(ms + EPS)
    partner = jnp.concatenate([vh[:, a:b] for a, b in perm_slices], axis=1)
    return vh * c_tab[None] + partner * s_tab[None]


_PERM_A = ((16, 32), (0, 16), (48, 64), (32, 48))
_PERM_B = ((8, 16), (0, 8), (16, 64))


def _inproj_kernel(x_ref, mod_ref, g_ref, wT_ref, tab_ref,
                   qaT_ref, vaT_ref, qbT_ref, vbT_ref, ka_ref, kb_ref):
    g = g_ref[...]
    mod = mod_ref[0]

    def finish(group, tok, pT):
        lanes = slice(tok, tok + IN_SUB)
        if group == "qa":
            r = _norm_rope_T(pT, HQ_A, tab_ref[0, :, lanes], tab_ref[1, :, lanes], _PERM_A)
            qaT_ref[0, :, 0, :, lanes] = r.astype(BF16)
        elif group == "qb":
            r = _norm_rope_T(pT, 2 * H_B, tab_ref[4, :, lanes], tab_ref[5, :, lanes], _PERM_B)
            qbT_ref[0, :, 0, :, lanes] = r.astype(BF16)
        elif group == "k":
            rb = _norm_rope_T(pT[:B_QK], 2 * H_B, tab_ref[6, :, lanes], tab_ref[7, :, lanes], _PERM_B)
            for hh in range(2 * H_B):
                kb_ref[0, hh, lanes, :] = rb[hh].T.astype(BF16)
            ra = _norm_rope_T(pT[B_QK:], HKV_A, tab_ref[2, :, lanes], tab_ref[3, :, lanes], _PERM_A)
            for hh in range(HKV_A):
                ka_ref[0, hh, lanes, :] = ra[hh].T.astype(BF16)
        else:
            vbT_ref[0, :, 0, :, lanes] = pT[:B_V].reshape(H_B, 2 * HEAD_DIM, IN_SUB).astype(BF16)
            vaT_ref[0, :, 0, :, lanes] = pT[B_V:].reshape(HKV_A, HEAD_DIM, IN_SUB).astype(BF16)

    rows = {"qa": (_R_QA, _R_QB), "qb": (_R_QB, _R_KB), "k": (_R_KB, _R_VB), "v": (_R_VB, _R_END)}
    n_sub = TS_IN // IN_SUB
    pending = []
    for sub in range(n_sub):
        tok = sub * IN_SUB
        h = _modulated_norm(x_ref[0, tok:tok + IN_SUB, :], g, mod).astype(BF16)
        order = ("k", "qa", "qb", "v") if sub == n_sub - 1 else ("qa", "v", "qb", "k")
        for name in order:
            lo, hi = rows[name]
            pending.append((name, tok, _nt_dot(wT_ref[lo:hi, :], h)))
            if len(pending) > IN_LOOKAHEAD:
                finish(*pending.pop(0))
    while pending:
        finish(*pending.pop(0))


def _inproj(x, mod3, norm_g, wT_att, tables):
    nb, s, d = x.shape
    nt = s // TS_IN
    out_shape = (
        jax.ShapeDtypeStruct((nb, HQ_A, nt, HEAD_DIM, TS_IN), BF16),
        jax.ShapeDtypeStruct((nb, HKV_A, nt, HEAD_DIM, TS_IN), BF16),
        jax.ShapeDtypeStruct((nb, 2 * H_B, nt, HEAD_DIM, TS_IN), BF16),
        jax.ShapeDtypeStruct((nb, H_B, nt, 2 * HEAD_DIM, TS_IN), BF16),
        jax.ShapeDtypeStruct((nb, HKV_A, s, HEAD_DIM), BF16),
        jax.ShapeDtypeStruct((nb, 2 * H_B, s, HEAD_DIM), BF16),
    )
    const = pl.Buffered(1)
    in_specs = [
        pl.BlockSpec((1, TS_IN, d), lambda t, b: (b, t, 0)),
        pl.BlockSpec((1, 3, d), lambda t, b: (b, 0, 0)),
        pl.BlockSpec((1, d), lambda t, b: (0, 0), pipeline_mode=const),
        pl.BlockSpec((_R_END, d), lambda t, b: (0, 0), pipeline_mode=const),
        pl.BlockSpec((8, HEAD_DIM, TS_IN), lambda t, b: (0, 0, t)),
    ]
    out_specs = (
        pl.BlockSpec((1, HQ_A, 1, HEAD_DIM, TS_IN), lambda t, b: (b, 0, t, 0, 0)),
        pl.BlockSpec((1, HKV_A, 1, HEAD_DIM, TS_IN), lambda t, b: (b, 0, t, 0, 0)),
        pl.BlockSpec((1, 2 * H_B, 1, HEAD_DIM, TS_IN), lambda t, b: (b, 0, t, 0, 0)),
        pl.BlockSpec((1, H_B, 1, 2 * HEAD_DIM, TS_IN), lambda t, b: (b, 0, t, 0, 0)),
        pl.BlockSpec((1, HKV_A, TS_IN, HEAD_DIM), lambda t, b: (b, 0, t, 0)),
        pl.BlockSpec((1, 2 * H_B, TS_IN, HEAD_DIM), lambda t, b: (b, 0, t, 0)),
    )
    return pl.pallas_call(
        _inproj_kernel,
        grid=(nt, nb),
        in_specs=in_specs,
        out_specs=out_specs,
        out_shape=out_shape,
        compiler_params=pltpu.CompilerParams(
            dimension_semantics=("arbitrary", "arbitrary"), vmem_limit_bytes=VMEM_LIMIT),
        name="in_proj",
    )(x, mod3, norm_g, wT_att, tables)


def _attend(n_maps, n_q, q_of, k_of, v_of, combine, emit, acc_ref, m_ref, l_ref, fin_ref, *, online):
    tq = acc_ref.shape[-1]
    n_chunks = k_of(0).shape[0] // TS_IN
    subs = TS_IN // KV_SUB

    def scores(t, i, c, w):
        off = c * TS_IN + w * KV_SUB
        if not isinstance(off, int):
            off = pl.multiple_of(off, KV_SUB)
        return jnp.dot(k_of(i)[pl.ds(off, KV_SUB), :], q_of(i, t), preferred_element_type=F32)

    def normalised(acc, l8):
        return acc * (1.0 / jnp.sum(l8, axis=0, keepdims=True))

    if online:
        def online_tile(t, carry):
            acc_ref[...] = jnp.zeros(acc_ref.shape, F32)
            l_ref[...] = jnp.zeros(l_ref.shape, F32)
            m_ref[...] = jnp.full(m_ref.shape, NEG_BIG, F32)

            def online_chunk(c, inner):
                for w in range(subs):
                    for i in range(n_maps):
                        s = scores(t, i, c, w)
                        m_prev = m_ref[i]
                        m_new = jnp.maximum(m_prev, jnp.max(s, axis=0, keepdims=True))
                        alpha = jnp.exp2(m_prev - m_new)
                        p = jnp.exp2(s - m_new)
                        l_ref[i] = alpha * l_ref[i] + p.reshape(KV_SUB // 8, 8, tq).sum(axis=0)
                        pv = jnp.dot(v_of(c, w), p.astype(BF16), preferred_element_type=F32)
                        acc_ref[i] = alpha * acc_ref[i] + pv
                        m_ref[i] = m_new
                return inner

            lax.fori_loop(0, n_chunks, online_chunk, 0)
            emit(t, combine([normalised(acc_ref[i], l_ref[i]) for i in range(n_maps)]))
            return carry

        lax.fori_loop(0, n_q, online_tile, 0)
        return

    items = [(i, c, w) for c in range(n_chunks) for w in range(subs) for i in range(n_maps)]
    fin_ref[...] = jnp.zeros(fin_ref.shape, F32)

    def raw_tile(t, carry):
        emit(jnp.maximum(t - 1, 0), fin_ref[...])
        l_part = [None] * n_maps
        pv_part = [None] * n_maps
        pending = []

        def consume(item, s):
            i, c, w = item
            p = jnp.exp2(s)
            lp = p.reshape(KV_SUB // 8, 8, tq).sum(axis=0)
            pv = jnp.dot(v_of(c, w), p.astype(BF16), preferred_element_type=F32)
            l_part[i] = lp if l_part[i] is None else l_part[i] + lp
            pv_part[i] = pv if pv_part[i] is None else pv_part[i] + pv

        for item in items:
            pending.append((item, scores(t, *item)))
            if len(pending) > SCORE_LOOKAHEAD:
                consume(*pending.pop(0))
        while pending:
            consume(*pending.pop(0))
        fin_ref[...] = combine([normalised(pv_part[i], l_part[i]) for i in range(n_maps)])
        return carry

    lax.fori_loop(0, n_q, raw_tile, 0)
    emit(n_q - 1, fin_ref[...])


def _attn_a_kernel(qT_ref, k_ref, vT_ref, o_ref, acc_ref, m_ref, l_ref, fin_ref, *, online):
    def emit(t, oT):
        rows = pl.ds(pl.multiple_of(t * TQ, TQ), TQ)
        o_ref[0, rows, :] = oT.T

    _attend(GROUP_A, qT_ref.shape[2], lambda i, t: qT_ref[0, i, t], lambda i: k_ref.at[0, 0],
            lambda c, w: vT_ref[0, 0, c, :, w * KV_SUB:(w + 1) * KV_SUB],
            lambda outs: jnp.concatenate(outs, axis=0), emit,
            acc_ref, m_ref, l_ref, fin_ref, online=online)


def _attn_b_kernel(lam_ref, qT_ref, k_ref, vT_ref, o_ref, acc_ref, m_ref, l_ref, fin_ref, *, online):
    lv = lam_ref[...]
    lam = (jnp.exp(jnp.sum(lv[0:1] * lv[1:2], axis=1, keepdims=True))
           - jnp.exp(jnp.sum(lv[2:3] * lv[3:4], axis=1, keepdims=True)) + LAM_INIT)

    def emit(t, oT):
        rows = pl.ds(pl.multiple_of(t * TQ, TQ), TQ)
        o_ref[0, rows, :] = oT.T

    _attend(2, qT_ref.shape[2], lambda i, t: qT_ref[0, i, t], lambda i: k_ref.at[0, i],
            lambda c, w: vT_ref[0, 0, c, :, w * KV_SUB:(w + 1) * KV_SUB],
            lambda outs: outs[0] - lam * outs[1], emit,
            acc_ref, m_ref, l_ref, fin_ref, online=online)


def _attn_a(qaT, ka, vaT, *, online):
    nb, _, nq, _, _ = qaT.shape
    s = nq * TQ
    nc = vaT.shape[2]
    width = GROUP_A * HEAD_DIM
    return pl.pallas_call(
        functools.partial(_attn_a_kernel, online=online),
        grid=(nb, HKV_A, nq // Q_TILES_PER_STEP),
        in_specs=[
            pl.BlockSpec((1, GROUP_A, Q_TILES_PER_STEP, HEAD_DIM, TQ), lambda b, g, i: (b, g, i, 0, 0)),
            pl.BlockSpec((1, 1, s, HEAD_DIM), lambda b, g, i: (b, g, 0, 0)),
            pl.BlockSpec((1, 1, nc, HEAD_DIM, TS_IN), lambda b, g, i: (b, g, 0, 0, 0)),
        ],
        out_specs=pl.BlockSpec((1, Q_TILES_PER_STEP * TQ, width), lambda b, g, i: (b, i, g)),
        out_shape=jax.ShapeDtypeStruct((nb, s, A_Q), F32),
        scratch_shapes=[
            pltpu.VMEM((GROUP_A, HEAD_DIM, TQ), F32),
            pltpu.VMEM((GROUP_A, 1, TQ), F32),
            pltpu.VMEM((GROUP_A, 8, TQ), F32),
            pltpu.VMEM((width, TQ), F32),
        ],
        compiler_params=pltpu.CompilerParams(
            dimension_semantics=("arbitrary", "arbitrary", "arbitrary"), vmem_limit_bytes=VMEM_LIMIT),
        name="attn_gqa_online" if online else "attn_gqa",
    )(qaT, ka, vaT)


def _attn_b(lam_vecs, qbT, kb, vbT, *, online):
    nb, _, nq, _, _ = qbT.shape
    s = nq * TQ
    nc = vbT.shape[2]
    dv = 2 * HEAD_DIM
    return pl.pallas_call(
        functools.partial(_attn_b_kernel, online=online),
        grid=(nb, H_B, nq // Q_TILES_PER_STEP),
        in_specs=[
            pl.BlockSpec((4, HEAD_DIM), lambda b, h, i: (0, 0)),
            pl.BlockSpec((1, 2, Q_TILES_PER_STEP, HEAD_DIM, TQ), lambda b, h, i: (b, h, i, 0, 0)),
            pl.BlockSpec((1, 2, s, HEAD_DIM), lambda b, h, i: (b, h, 0, 0)),
            pl.BlockSpec((1, 1, nc, dv, TS_IN), lambda b, h, i: (b, h, 0, 0, 0)),
        ],
        out_specs=pl.BlockSpec((1, Q_TILES_PER_STEP * TQ, dv), lambda b, h, i: (b, i, h)),
        out_shape=jax.ShapeDtypeStruct((nb, s, B_V), F32),
        scratch_shapes=[
            pltpu.VMEM((2, dv, TQ), F32),
            pltpu.VMEM((2, 1, TQ), F32),
            pltpu.VMEM((2, 8, TQ), F32),
            pltpu.VMEM((dv, TQ), F32),
        ],
        compiler_params=pltpu.CompilerParams(
            dimension_semantics=("arbitrary", "arbitrary", "arbitrary"), vmem_limit_bytes=VMEM_LIMIT),
        name="attn_diff_online" if online else "attn_diff",
    )(lam_vecs, qbT, kb, vbT)


def _out_kernel(x_ref, mod_ref, g_ref, oa_ref, ob_ref, wg_ref, wpa_ref, wpb_ref, wo_ref, sg_ref, y_ref):
    x = x_ref[0]
    mod = mod_ref[0]
    h = _modulated_norm(x, g_ref[...], mod).astype(BF16)
    gates = jnp.dot(h, wg_ref[...], preferred_element_type=F32)
    za = gates[:, 0:A_Q]
    zb = gates[:, A_Q:A_Q + B_V]
    ga = gates[:, A_Q + B_V:A_Q + B_V + D_MODEL]
    gb = gates[:, A_Q + B_V + D_MODEL:]
    a = oa_ref[0] * (za * _sigmoid(za))
    pa = jnp.dot(a.astype(BF16), wpa_ref[...], preferred_element_type=F32)
    dv = 2 * HEAD_DIM
    ob = ob_ref[0]
    sg = sg_ref[...] * (1.0 - LAM_INIT)
    normed = []
    for hh in range(H_B):
        oh = ob[:, hh * dv:(hh + 1) * dv]
        ms = jnp.mean(oh * oh, axis=-1, keepdims=True)
        normed.append(oh * lax.rsqrt(ms + EPS) * sg)
    bn = jnp.concatenate(normed, axis=-1) * (zb * _sigmoid(zb))
    pb = jnp.dot(bn.astype(BF16), wpb_ref[...], preferred_element_type=F32)
    merged = _sigmoid(ga) * pa + _sigmoid(gb) * pb
    out = jnp.dot(merged.astype(BF16), wo_ref[...], preferred_element_type=F32)
    y_ref[0] = x + mod[2:3] * out


def _out_stage(x, mod3, norm_g, oa, ob, w_gates, w_pa, w_pb, w_out, subln_g):
    nb, s, d = x.shape
    const = pl.Buffered(1)
    n_gate = w_gates.shape[1]
    return pl.pallas_call(
        _out_kernel,
        grid=(nb, s // TS_OUT),
        in_specs=[
            pl.BlockSpec((1, TS_OUT, d), lambda b, t: (b, t, 0)),
            pl.BlockSpec((1, 3, d), lambda b, t: (b, 0, 0)),
            pl.BlockSpec((1, d), lambda b, t: (0, 0), pipeline_mode=const),
            pl.BlockSpec((1, TS_OUT, A_Q), lambda b, t: (b, t, 0)),
            pl.BlockSpec((1, TS_OUT, B_V), lambda b, t: (b, t, 0)),
            pl.BlockSpec((d, n_gate), lambda b, t: (0, 0), pipeline_mode=const),
            pl.BlockSpec((A_Q, d), lambda b, t: (0, 0), pipeline_mode=const),
            pl.BlockSpec((B_V, d), lambda b, t: (0, 0), pipeline_mode=const),
            pl.BlockSpec((d, d), lambda b, t: (0, 0), pipeline_mode=const),
            pl.BlockSpec((1, 2 * HEAD_DIM), lambda b, t: (0, 0), pipeline_mode=const),
        ],
        out_specs=pl.BlockSpec((1, TS_OUT, d), lambda b, t: (b, t, 0)),
        out_shape=jax.ShapeDtypeStruct((nb, s, d), F32),
        compiler_params=pltpu.CompilerParams(
            dimension_semantics=("arbitrary", "arbitrary"), vmem_limit_bytes=VMEM_LIMIT),
        name="out_stage",
    )(x, mod3, norm_g, oa, ob, w_gates, w_pa, w_pb, w_out, subln_g)


def _rope_coeff_tables(qn_a, kn_a, qn_b, kn_b, s):
    t = jnp.arange(s)
    row = (t // GRID_W).astype(F32)
    col = (t % GRID_W).astype(F32)
    inv_ax = AX_THETA ** (-jnp.arange(0, AX_DIM, 2, dtype=F32) / AX_DIM)
    ang_r = inv_ax[:, None] * row[None, :]
    ang_c = inv_ax[:, None] * col[None, :]
    ang_a = jnp.concatenate([ang_r, ang_r, ang_c, ang_c], axis=0)
    cos_a, sin_a = jnp.cos(ang_a), jnp.sin(ang_a)
    half = AX_DIM // 2
    sign_a = jnp.concatenate([-jnp.ones(half), jnp.ones(half), -jnp.ones(half), jnp.ones(half)]).astype(F32)
    perm_a = jnp.concatenate([jnp.arange(a, b) for a, b in _PERM_A])

    inv_p = ROPE_THETA ** (-jnp.arange(0, ROT_DIM, 2, dtype=F32) / ROT_DIM)
    ang_p = inv_p[:, None] * t.astype(F32)[None, :]
    ang_p = jnp.concatenate([ang_p, ang_p], axis=0)
    rest = HEAD_DIM - ROT_DIM
    cos_b = jnp.concatenate([jnp.cos(ang_p), jnp.ones((rest, s), F32)], axis=0)
    sin_b = jnp.concatenate([jnp.sin(ang_p), jnp.zeros((rest, s), F32)], axis=0)
    hb = ROT_DIM // 2
    sign_b = jnp.concatenate([-jnp.ones(hb), jnp.ones(hb), jnp.zeros(rest)]).astype(F32)
    perm_b = jnp.concatenate([jnp.arange(a, b) for a, b in _PERM_B])

    def pair(g, cos, sin, sign, perm, scale):
        g = g.astype(F32)
        return [scale * g[:, None] * cos, scale * (sign * g[perm])[:, None] * sin]

    tabs = (pair(qn_a, cos_a, sin_a, sign_a, perm_a, Q_SCALE) + pair(kn_a, cos_a, sin_a, sign_a, perm_a, 1.0)
            + pair(qn_b, cos_b, sin_b, sign_b, perm_b, Q_SCALE) + pair(kn_b, cos_b, sin_b, sign_b, perm_b, 1.0))
    return jnp.stack(tabs, axis=0)


def _layer(x, c, wts, tables, safe):
    (w_ada_bf, b_ada, norm_g, wT_att, w_gates, w_pa, w_pb, w_out, subln_g, lam_vecs) = wts
    nb = x.shape[0]
    mod3 = _ada_mod(c, w_ada_bf, b_ada).reshape(nb, 3, D_MODEL)
    qaT, vaT, qbT, vbT, ka, kb = _inproj(x, mod3, norm_g, wT_att, tables)

    def attend(online):
        def run(ops):
            qaT_, ka_, vaT_, qbT_, kb_, vbT_ = ops
            return (_attn_a(qaT_, ka_, vaT_, online=online),
                    _attn_b(lam_vecs, qbT_, kb_, vbT_, online=online))
        return run

    oa, ob = lax.cond(safe, attend(False), attend(True), (qaT, ka, vaT, qbT, kb, vbT))
    return _out_stage(x, mod3, norm_g, oa, ob, w_gates, w_pa, w_pb, w_out, subln_g)


def kernel(x_prompt, x_sample, c_prompt, c_sample, w_ada, b_ada, norm_g, w_in, qn_a, kn_a, qn_b, kn_b,
           lam_q1, lam_k1, lam_q2, lam_k2, subln_g, w_proj_a, w_proj_b, w_out):
    w = w_in[0]
    cols = lambda a, b: w[:, a:b]
    wT_att = jnp.concatenate(
        [cols(_O_QA, _O_KA), cols(_O_QB, _O_KB), cols(_O_KB, _O_VB), cols(_O_KA, _O_VA),
         cols(_O_VB, _O_ZB), cols(_O_VA, _O_ZA)], axis=1).T.astype(BF16)
    w_gates = jnp.concatenate(
        [cols(_O_ZA, _O_QB), cols(_O_ZB, _O_GA), cols(_O_GA, _O_END)], axis=1).astype(BF16)
    wts = (w_ada[0].astype(BF16), b_ada[0][None, :], norm_g[0][None, :], wT_att, w_gates,
           w_proj_a[0].astype(BF16), w_proj_b[0].astype(BF16), w_out[0].astype(BF16),
           subln_g[0][None, :], jnp.stack([lam_q1[0], lam_k1[0], lam_q2[0], lam_k2[0]], axis=0).astype(F32))
    tables = _rope_coeff_tables(qn_a[0], kn_a[0], qn_b[0], kn_b[0], SEQ)
    bound_a = math.sqrt(HEAD_DIM) * jnp.max(jnp.abs(qn_a[0])) * jnp.max(jnp.abs(kn_a[0]))
    bound_b = math.sqrt(HEAD_DIM) * jnp.max(jnp.abs(qn_b[0])) * jnp.max(jnp.abs(kn_b[0]))
    safe = jnp.maximum(bound_a, bound_b) <= SAFE_SCORE_BOUND
    y_prompt = _layer(x_prompt, c_prompt, wts, tables, safe)
    y_sample = _layer(x_sample, c_sample, wts, tables, safe)
    return (y_prompt, y_sample)
```

```python
import functools
import math

import jax
import jax.numpy as jnp
from jax import lax
from jax.experimental import pallas as pl
from jax.experimental.pallas import tpu as pltpu

F32 = jnp.float32
BF16 = jnp.bfloat16
F8 = jnp.float8_e4m3fn

D_MODEL = 1024
SEQ = 8192
GRID_W = 64
HEAD_DIM = 64
HQ_A = 8
HKV_A = 2
GROUP_A = HQ_A // HKV_A
AX_DIM = HEAD_DIM // 2
AX_THETA = 10000.0
A_Q = HQ_A * HEAD_DIM
A_KV = HKV_A * HEAD_DIM
H_B = 4
B_QK = H_B * 2 * HEAD_DIM
B_V = H_B * 2 * HEAD_DIM
ROT_DIM = HEAD_DIM // 4
ROPE_THETA = 500000.0
EPS = 1e-6
LOG2E = math.log2(math.e)
Q_SCALE = LOG2E / math.sqrt(HEAD_DIM)
LAM_INIT = 0.8 - 0.6 * math.exp(-0.3 * 0)
SPLIT_DIM = 4 * HEAD_DIM
F8_TARGET_MAX = 128.0

_O_QA = 0
_O_KA = _O_QA + A_Q
_O_VA = _O_KA + A_KV
_O_ZA = _O_VA + A_KV
_O_QB = _O_ZA + A_Q
_O_KB = _O_QB + B_QK
_O_VB = _O_KB + B_QK
_O_ZB = _O_VB + B_V
_O_GA = _O_ZB + B_V
_O_GB = _O_GA + D_MODEL
_O_END = _O_GB + D_MODEL

_R_QA = 0
_R_QB = _R_QA + A_Q
_R_KB = _R_QB + B_QK
_R_KA = _R_KB + B_QK
_R_VB = _R_KA + A_KV
_R_VA = _R_VB + B_V
_R_END = _R_VA + A_KV

CHUNK = 512
TS_IN = 512
IN_SUB = 256
IN_LOOKAHEAD = 1
TS_OUT = 512
TQ = CHUNK
Q_TILES_PER_STEP = 16
KV_SUB = 256
SCORE_LOOKAHEAD_A = 3
SCORE_LOOKAHEAD_B = 2
VMEM_LIMIT = 56 * 1024 * 1024
SAFE_SCORE_BOUND = 40.0
NEG_BIG = -1e30
assert TS_IN % CHUNK == 0 and CHUNK % IN_SUB == 0 and CHUNK % KV_SUB == 0


def _nt_dot(a, b):
    return lax.dot_general(a, b, (((1,), (1,)), ((), ())), preferred_element_type=F32)


def _sigmoid(x):
    return 1.0 / (1.0 + jnp.exp(-x))


def _modulated_norm(x, g, mod):
    ms = jnp.mean(x * x, axis=-1, keepdims=True)
    return (x * lax.rsqrt(ms + EPS)) * (g * (1.0 + mod[1:2])) + mod[0:1]


def _ada_kernel(c_ref, w_ref, b_ref, o_ref):
    c = c_ref[...]
    s = (c * _sigmoid(c)).astype(BF16)
    o_ref[...] = jnp.dot(s, w_ref[...], preferred_element_type=F32) + b_ref[...]


def _ada_mod(c, w_ada_bf, b_ada):
    nb = c.shape[0]
    return pl.pallas_call(
        _ada_kernel,
        out_shape=jax.ShapeDtypeStruct((nb, 3 * D_MODEL), F32),
        compiler_params=pltpu.CompilerParams(vmem_limit_bytes=VMEM_LIMIT),
        name="ada_mod",
    )(c, w_ada_bf, b_ada)


def _norm_rope_T(rows, n_heads, c_tab, s_tab, perm_slices):
    ts = rows.shape[-1]
    v = rows.reshape(n_heads, HEAD_DIM, ts)
    ms = jnp.mean(v * v, axis=1, keepdims=True)
    vh = v * lax.rsqrt(ms + EPS)
    partner = jnp.concatenate([vh[:, a:b] for a, b in perm_slices], axis=1)
    return vh * c_tab[None] + partner * s_tab[None]


_PERM_A = ((16, 32), (0, 16), (48, 64), (32, 48))
_PERM_B = ((8, 16), (0, 8), (16, 64))


def _split_fp8(r):
    hi = r.astype(F8)
    hi32 = hi.astype(F32)
    return hi, hi32, r - hi32


def _store_split_q(q_ref, ck, lanes, r):
    hi, _, lo32 = _split_fp8(r)
    lo = lo32.astype(F8)
    for part, val in enumerate((hi, lo, hi, lo)):
        q_ref[0, :, ck, part * HEAD_DIM:(part + 1) * HEAD_DIM, lanes] = val


def _store_split_k(k_ref, toks, r):
    _, hi32, lo32 = _split_fp8(r)
    split = jnp.concatenate([hi32, hi32, lo32, lo32], axis=1)
    for hh in range(r.shape[0]):
        k_ref[0, hh, toks, :] = split[hh].T.astype(F8)


def _inproj_kernel(x_ref, mod_ref, g_ref, wT_ref, tab_ref,
                   qaT_ref, vaT_ref, qbT_ref, vbT_ref, ka_ref, kb_ref):
    g = g_ref[...]
    mod = mod_ref[0]

    def finish(group, tok, pT):
        toks = slice(tok, tok + IN_SUB)
        ck, lanes = tok // CHUNK, slice(tok % CHUNK, tok % CHUNK + IN_SUB)
        if group == "qa":
            r = _norm_rope_T(pT, HQ_A, tab_ref[0, :, toks], tab_ref[1, :, toks], _PERM_A)
            _store_split_q(qaT_ref, ck, lanes, r)
        elif group == "qb":
            r = _norm_rope_T(pT, 2 * H_B, tab_ref[4, :, toks], tab_ref[5, :, toks], _PERM_B)
            _store_split_q(qbT_ref, ck, lanes, r)
        elif group == "k":
            rb = _norm_rope_T(pT[:B_QK], 2 * H_B, tab_ref[6, :, toks], tab_ref[7, :, toks], _PERM_B)
            _store_split_k(kb_ref, toks, rb)
            ra = _norm_rope_T(pT[B_QK:], HKV_A, tab_ref[2, :, toks], tab_ref[3, :, toks], _PERM_A)
            _store_split_k(ka_ref, toks, ra)
        else:
            vbT_ref[0, :, ck, :, lanes] = pT[:B_V].reshape(H_B, 2 * HEAD_DIM, IN_SUB).astype(BF16)
            vaT_ref[0, :, ck, :, lanes] = pT[B_V:].reshape(HKV_A, HEAD_DIM, IN_SUB).astype(BF16)

    rows = {"qa": (_R_QA, _R_QB), "qb": (_R_QB, _R_KB), "k": (_R_KB, _R_VB), "v": (_R_VB, _R_END)}
    n_sub = TS_IN // IN_SUB
    pending = []
    for sub in range(n_sub):
        tok = sub * IN_SUB
        h = _modulated_norm(x_ref[0, tok:tok + IN_SUB, :], g, mod).astype(BF16)
        order = ("k", "qa", "qb", "v") if sub == n_sub - 1 else ("qa", "v", "qb", "k")
        for name in order:
            lo, hi = rows[name]
            pending.append((name, tok, _nt_dot(wT_ref[lo:hi, :], h)))
            if len(pending) > IN_LOOKAHEAD:
                finish(*pending.pop(0))
    while pending:
        finish(*pending.pop(0))


def _inproj(x, mod3, norm_g, wT_att, tables):
    nb, s, d = x.shape
    nt = s // TS_IN
    nc = s // CHUNK
    cpt = TS_IN // CHUNK
    out_shape = (
        jax.ShapeDtypeStruct((nb, HQ_A, nc, SPLIT_DIM, CHUNK), F8),
        jax.ShapeDtypeStruct((nb, HKV_A, nc, HEAD_DIM, CHUNK), BF16),
        jax.ShapeDtypeStruct((nb, 2 * H_B, nc, SPLIT_DIM, CHUNK), F8),
        jax.ShapeDtypeStruct((nb, H_B, nc, 2 * HEAD_DIM, CHUNK), BF16),
        jax.ShapeDtypeStruct((nb, HKV_A, s, SPLIT_DIM), F8),
        jax.ShapeDtypeStruct((nb, 2 * H_B, s, SPLIT_DIM), F8),
    )
    const = pl.Buffered(1)
    in_specs = [
        pl.BlockSpec((1, TS_IN, d), lambda t, b: (b, t, 0)),
        pl.BlockSpec((1, 3, d), lambda t, b: (b, 0, 0)),
        pl.BlockSpec((1, d), lambda t, b: (0, 0), pipeline_mode=const),
        pl.BlockSpec((_R_END, d), lambda t, b: (0, 0), pipeline_mode=const),
        pl.BlockSpec((8, HEAD_DIM, TS_IN), lambda t, b: (0, 0, t)),
    ]
    out_specs = (
        pl.BlockSpec((1, HQ_A, cpt, SPLIT_DIM, CHUNK), lambda t, b: (b, 0, t, 0, 0)),
        pl.BlockSpec((1, HKV_A, cpt, HEAD_DIM, CHUNK), lambda t, b: (b, 0, t, 0, 0)),
        pl.BlockSpec((1, 2 * H_B, cpt, SPLIT_DIM, CHUNK), lambda t, b: (b, 0, t, 0, 0)),
        pl.BlockSpec((1, H_B, cpt, 2 * HEAD_DIM, CHUNK), lambda t, b: (b, 0, t, 0, 0)),
        pl.BlockSpec((1, HKV_A, TS_IN, SPLIT_DIM), lambda t, b: (b, 0, t, 0)),
        pl.BlockSpec((1, 2 * H_B, TS_IN, SPLIT_DIM), lambda t, b: (b, 0, t, 0)),
    )
    return pl.pallas_call(
        _inproj_kernel,
        grid=(nt, nb),
        in_specs=in_specs,
        out_specs=out_specs,
        out_shape=out_shape,
        compiler_params=pltpu.CompilerParams(
            dimension_semantics=("arbitrary", "arbitrary"), vmem_limit_bytes=VMEM_LIMIT),
        name="in_proj",
    )(x, mod3, norm_g, wT_att, tables)


def _attend(n_maps, n_q, q_of, k_of, v_of, unscale, combine, emit, acc_ref, m_ref, l_ref, fin_ref, *,
            online, lookahead):
    tq = acc_ref.shape[-1]
    n_chunks = k_of(0).shape[0] // CHUNK
    subs = CHUNK // KV_SUB

    def scores(t, i, c, w):
        off = c * CHUNK + w * KV_SUB
        if not isinstance(off, int):
            off = pl.multiple_of(off, KV_SUB)
        s = jnp.dot(k_of(i)[pl.ds(off, KV_SUB), :], q_of(i, t), preferred_element_type=F32)
        return s * unscale if online else s

    def normalised(acc, l8):
        return acc * (1.0 / jnp.sum(l8, axis=0, keepdims=True))

    if online:
        def online_tile(t, carry):
            acc_ref[...] = jnp.zeros(acc_ref.shape, F32)
            l_ref[...] = jnp.zeros(l_ref.shape, F32)
            m_ref[...] = jnp.full(m_ref.shape, NEG_BIG, F32)

            def online_chunk(c, inner):
                for w in range(subs):
                    for i in range(n_maps):
                        s = scores(t, i, c, w)
                        m_prev = m_ref[i]
                        m_new = jnp.maximum(m_prev, jnp.max(s, axis=0, keepdims=True))
                        alpha = jnp.exp2(m_prev - m_new)
                        p = jnp.exp2(s - m_new)
                        l_ref[i] = alpha * l_ref[i] + p.reshape(KV_SUB // 8, 8, tq).sum(axis=0)
                        pv = jnp.dot(v_of(c, w), p.astype(BF16), preferred_element_type=F32)
                        acc_ref[i] = alpha * acc_ref[i] + pv
                        m_ref[i] = m_new
                return inner

            lax.fori_loop(0, n_chunks, online_chunk, 0)
            emit(t, combine([normalised(acc_ref[i], l_ref[i]) for i in range(n_maps)]))
            return carry

        lax.fori_loop(0, n_q, online_tile, 0)
        return

    items = [(i, c, w) for c in range(n_chunks) for w in range(subs) for i in range(n_maps)]
    fin_ref[...] = jnp.zeros(fin_ref.shape, F32)

    def raw_tile(t, carry):
        emit(jnp.maximum(t - 1, 0), fin_ref[...])
        l_part = [None] * n_maps
        pv_part = [None] * n_maps
        pending = []

        def consume(item, s):
            i, c, w = item
            p = jnp.exp2(s)
            lp = p.reshape(KV_SUB // 8, 8, tq).sum(axis=0)
            pv = jnp.dot(v_of(c, w), p.astype(BF16), preferred_element_type=F32)
            l_part[i] = lp if l_part[i] is None else l_part[i] + lp
            pv_part[i] = pv if pv_part[i] is None else pv_part[i] + pv

        for item in items:
            pending.append((item, scores(t, *item)))
            if len(pending) > lookahead:
                consume(*pending.pop(0))
        while pending:
            consume(*pending.pop(0))
        fin_ref[...] = combine([normalised(pv_part[i], l_part[i]) for i in range(n_maps)])
        return carry

    lax.fori_loop(0, n_q, raw_tile, 0)
    emit(n_q - 1, fin_ref[...])


def _attn_a_kernel(us_ref, qT_ref, k_ref, vT_ref, o_ref, acc_ref, m_ref, l_ref, fin_ref, *, online):
    def emit(t, oT):
        rows = pl.ds(pl.multiple_of(t * TQ, TQ), TQ)
        o_ref[0, rows, :] = oT.T

    _attend(GROUP_A, qT_ref.shape[2], lambda i, t: qT_ref[0, i, t], lambda i: k_ref.at[0, 0],
            lambda c, w: vT_ref[0, 0, c, :, w * KV_SUB:(w + 1) * KV_SUB], us_ref[0],
            lambda outs: jnp.concatenate(outs, axis=0), emit,
            acc_ref, m_ref, l_ref, fin_ref, online=online, lookahead=SCORE_LOOKAHEAD_A)


def _attn_b_kernel(us_ref, lam_ref, qT_ref, k_ref, vT_ref, o_ref, acc_ref, m_ref, l_ref, fin_ref, *, online):
    lv = lam_ref[...]
    lam = (jnp.exp(jnp.sum(lv[0:1] * lv[1:2], axis=1, keepdims=True))
           - jnp.exp(jnp.sum(lv[2:3] * lv[3:4], axis=1, keepdims=True)) + LAM_INIT)

    def emit(t, oT):
        rows = pl.ds(pl.multiple_of(t * TQ, TQ), TQ)
        o_ref[0, rows, :] = oT.T

    _attend(2, qT_ref.shape[2], lambda i, t: qT_ref[0, i, t], lambda i: k_ref.at[0, i],
            lambda c, w: vT_ref[0, 0, c, :, w * KV_SUB:(w + 1) * KV_SUB], us_ref[0],
            lambda outs: outs[0] - lam * outs[1], emit,
            acc_ref, m_ref, l_ref, fin_ref, online=online, lookahead=SCORE_LOOKAHEAD_B)


def _attn_a(unscale, qaT, ka, vaT, *, online):
    nb, _, nq, _, _ = qaT.shape
    s = nq * TQ
    nc = vaT.shape[2]
    width = GROUP_A * HEAD_DIM
    return pl.pallas_call(
        functools.partial(_attn_a_kernel, online=online),
        grid=(nb, HKV_A, nq // Q_TILES_PER_STEP),
        in_specs=[
            pl.BlockSpec(memory_space=pltpu.SMEM),
            pl.BlockSpec((1, GROUP_A, Q_TILES_PER_STEP, SPLIT_DIM, TQ), lambda b, g, i: (b, g, i, 0, 0)),
            pl.BlockSpec((1, 1, s, SPLIT_DIM), lambda b, g, i: (b, g, 0, 0)),
            pl.BlockSpec((1, 1, nc, HEAD_DIM, CHUNK), lambda b, g, i: (b, g, 0, 0, 0)),
        ],
        out_specs=pl.BlockSpec((1, Q_TILES_PER_STEP * TQ, width), lambda b, g, i: (b, i, g)),
        out_shape=jax.ShapeDtypeStruct((nb, s, A_Q), F32),
        scratch_shapes=[
            pltpu.VMEM((GROUP_A, HEAD_DIM, TQ), F32),
            pltpu.VMEM((GROUP_A, 1, TQ), F32),
            pltpu.VMEM((GROUP_A, 8, TQ), F32),
            pltpu.VMEM((width, TQ), F32),
        ],
        compiler_params=pltpu.CompilerParams(
            dimension_semantics=("arbitrary", "arbitrary", "arbitrary"), vmem_limit_bytes=VMEM_LIMIT),
        name="attn_gqa_online" if online else "attn_gqa",
    )(unscale, qaT, ka, vaT)


def _attn_b(unscale, lam_vecs, qbT, kb, vbT, *, online):
    nb, _, nq, _, _ = qbT.shape
    s = nq * TQ
    nc = vbT.shape[2]
    dv = 2 * HEAD_DIM
    return pl.pallas_call(
        functools.partial(_attn_b_kernel, online=online),
        grid=(nb, H_B, nq // Q_TILES_PER_STEP),
        in_specs=[
            pl.BlockSpec(memory_space=pltpu.SMEM),
            pl.BlockSpec((4, HEAD_DIM), lambda b, h, i: (0, 0)),
            pl.BlockSpec((1, 2, Q_TILES_PER_STEP, SPLIT_DIM, TQ), lambda b, h, i: (b, h, i, 0, 0)),
            pl.BlockSpec((1, 2, s, SPLIT_DIM), lambda b, h, i: (b, h, 0, 0)),
            pl.BlockSpec((1, 1, nc, dv, CHUNK), lambda b, h, i: (b, h, 0, 0, 0)),
        ],
        out_specs=pl.BlockSpec((1, Q_TILES_PER_STEP * TQ, dv), lambda b, h, i: (b, i, h)),
        out_shape=jax.ShapeDtypeStruct((nb, s, B_V), F32),
        scratch_shapes=[
            pltpu.VMEM((2, dv, TQ), F32),
            pltpu.VMEM((2, 1, TQ), F32),
            pltpu.VMEM((2, 8, TQ), F32),
            pltpu.VMEM((dv, TQ), F32),
        ],
        compiler_params=pltpu.CompilerParams(
            dimension_semantics=("arbitrary", "arbitrary", "arbitrary"), vmem_limit_bytes=VMEM_LIMIT),
        name="attn_diff_online" if online else "attn_diff",
    )(unscale, lam_vecs, qbT, kb, vbT)


def _out_kernel(x_ref, mod_ref, g_ref, oa_ref, ob_ref, wg_ref, wpa_ref, wpb_ref, wo_ref, sg_ref, y_ref):
    x = x_ref[0]
    mod = mod_ref[0]
    h = _modulated_norm(x, g_ref[...], mod).astype(BF16)
    gates = jnp.dot(h, wg_ref[...], preferred_element_type=F32)
    za = gates[:, 0:A_Q]
    zb = gates[:, A_Q:A_Q + B_V]
    ga = gates[:, A_Q + B_V:A_Q + B_V + D_MODEL]
    gb = gates[:, A_Q + B_V + D_MODEL:]
    a = oa_ref[0] * (za * _sigmoid(za))
    pa = jnp.dot(a.astype(BF16), wpa_ref[...], preferred_element_type=F32)
    dv = 2 * HEAD_DIM
    ob = ob_ref[0]
    sg = sg_ref[...] * (1.0 - LAM_INIT)
    normed = []
    for hh in range(H_B):
        oh = ob[:, hh * dv:(hh + 1) * dv]
        ms = jnp.mean(oh * oh, axis=-1, keepdims=True)
        normed.append(oh * lax.rsqrt(ms + EPS) * sg)
    bn = jnp.concatenate(normed, axis=-1) * (zb * _sigmoid(zb))
    pb = jnp.dot(bn.astype(BF16), wpb_ref[...], preferred_element_type=F32)
    merged = _sigmoid(ga) * pa + _sigmoid(gb) * pb
    out = jnp.dot(merged.astype(BF16), wo_ref[...], preferred_element_type=F32)
    y_ref[0] = x + mod[2:3] * out


def _out_stage(x, mod3, norm_g, oa, ob, w_gates, w_pa, w_pb, w_out, subln_g):
    nb, s, d = x.shape
    const = pl.Buffered(1)
    n_gate = w_gates.shape[1]
    return pl.pallas_call(
        _out_kernel,
        grid=(nb, s // TS_OUT),
        in_specs=[
            pl.BlockSpec((1, TS_OUT, d), lambda b, t: (b, t, 0)),
            pl.BlockSpec((1, 3, d), lambda b, t: (b, 0, 0)),
            pl.BlockSpec((1, d), lambda b, t: (0, 0), pipeline_mode=const),
            pl.BlockSpec((1, TS_OUT, A_Q), lambda b, t: (b, t, 0)),
            pl.BlockSpec((1, TS_OUT, B_V), lambda b, t: (b, t, 0)),
            pl.BlockSpec((d, n_gate), lambda b, t: (0, 0), pipeline_mode=const),
            pl.BlockSpec((A_Q, d), lambda b, t: (0, 0), pipeline_mode=const),
            pl.BlockSpec((B_V, d), lambda b, t: (0, 0), pipeline_mode=const),
            pl.BlockSpec((d, d), lambda b, t: (0, 0), pipeline_mode=const),
            pl.BlockSpec((1, 2 * HEAD_DIM), lambda b, t: (0, 0), pipeline_mode=const),
        ],
        out_specs=pl.BlockSpec((1, TS_OUT, d), lambda b, t: (b, t, 0)),
        out_shape=jax.ShapeDtypeStruct((nb, s, d), F32),
        compiler_params=pltpu.CompilerParams(
            dimension_semantics=("arbitrary", "arbitrary"), vmem_limit_bytes=VMEM_LIMIT),
        name="out_stage",
    )(x, mod3, norm_g, oa, ob, w_gates, w_pa, w_pb, w_out, subln_g)


def _rope_coeff_tables(qn_a, kn_a, qn_b, kn_b, fp8_scales, s):
    t = jnp.arange(s)
    row = (t // GRID_W).astype(F32)
    col = (t % GRID_W).astype(F32)
    inv_ax = AX_THETA ** (-jnp.arange(0, AX_DIM, 2, dtype=F32) / AX_DIM)
    ang_r = inv_ax[:, None] * row[None, :]
    ang_c = inv_ax[:, None] * col[None, :]
    ang_a = jnp.concatenate([ang_r, ang_r, ang_c, ang_c], axis=0)
    cos_a, sin_a = jnp.cos(ang_a), jnp.sin(ang_a)
    half = AX_DIM // 2
    sign_a = jnp.concatenate([-jnp.ones(half), jnp.ones(half), -jnp.ones(half), jnp.ones(half)]).astype(F32)
    perm_a = jnp.concatenate([jnp.arange(a, b) for a, b in _PERM_A])

    inv_p = ROPE_THETA ** (-jnp.arange(0, ROT_DIM, 2, dtype=F32) / ROT_DIM)
    ang_p = inv_p[:, None] * t.astype(F32)[None, :]
    ang_p = jnp.concatenate([ang_p, ang_p], axis=0)
    rest = HEAD_DIM - ROT_DIM
    cos_b = jnp.concatenate([jnp.cos(ang_p), jnp.ones((rest, s), F32)], axis=0)
    sin_b = jnp.concatenate([jnp.sin(ang_p), jnp.zeros((rest, s), F32)], axis=0)
    hb = ROT_DIM // 2
    sign_b = jnp.concatenate([-jnp.ones(hb), jnp.ones(hb), jnp.zeros(rest)]).astype(F32)
    perm_b = jnp.concatenate([jnp.arange(a, b) for a, b in _PERM_B])

    def pair(g, cos, sin, sign, perm, scale):
        g = g.astype(F32)
        return [scale * g[:, None] * cos, scale * (sign * g[perm])[:, None] * sin]

    sq_a, sk_a, sq_b, sk_b = fp8_scales
    tabs = (pair(qn_a, cos_a, sin_a, sign_a, perm_a, Q_SCALE * sq_a) + pair(kn_a, cos_a, sin_a, sign_a, perm_a, sk_a)
            + pair(qn_b, cos_b, sin_b, sign_b, perm_b, Q_SCALE * sq_b) + pair(kn_b, cos_b, sin_b, sign_b, perm_b, sk_b))
    return jnp.stack(tabs, axis=0)


def _fp8_scale_exponents(q_gain, k_gain, safe):
    bq = math.sqrt(HEAD_DIM) * jnp.max(jnp.abs(q_gain.astype(F32))) * Q_SCALE
    bk = math.sqrt(HEAD_DIM) * jnp.max(jnp.abs(k_gain.astype(F32)))
    lim = lambda e: jnp.clip(jnp.nan_to_num(e), -40.0, 40.0)
    e_bal = lim(jnp.round(0.5 * jnp.log2(bk / bq)))
    e_q = jnp.where(safe, e_bal, lim(jnp.floor(jnp.log2(F8_TARGET_MAX / bq))))
    e_k = jnp.where(safe, -e_bal, lim(jnp.floor(jnp.log2(F8_TARGET_MAX / bk))))
    return e_q.astype(jnp.int32), e_k.astype(jnp.int32)


def _layer(x, c, wts, tables, unscales, safe):
    (w_ada_bf, b_ada, norm_g, wT_att, w_gates, w_pa, w_pb, w_out, subln_g, lam_vecs) = wts
    us_a, us_b = unscales
    nb = x.shape[0]
    mod3 = _ada_mod(c, w_ada_bf, b_ada).reshape(nb, 3, D_MODEL)
    qaT, vaT, qbT, vbT, ka, kb = _inproj(x, mod3, norm_g, wT_att, tables)

    def attend(online):
        def run(ops):
            qaT_, ka_, vaT_, qbT_, kb_, vbT_ = ops
            return (_attn_a(us_a, qaT_, ka_, vaT_, online=online),
                    _attn_b(us_b, lam_vecs, qbT_, kb_, vbT_, online=online))
        return run

    oa, ob = lax.cond(safe, attend(False), attend(True), (qaT, ka, vaT, qbT, kb, vbT))
    return _out_stage(x, mod3, norm_g, oa, ob, w_gates, w_pa, w_pb, w_out, subln_g)


def kernel(x_prompt, x_sample, c_prompt, c_sample, w_ada, b_ada, norm_g, w_in, qn_a, kn_a, qn_b, kn_b,
           lam_q1, lam_k1, lam_q2, lam_k2, subln_g, w_proj_a, w_proj_b, w_out):
    w = w_in[0]
    cols = lambda a, b: w[:, a:b]
    wT_att = jnp.concatenate(
        [cols(_O_QA, _O_KA), cols(_O_QB, _O_KB), cols(_O_KB, _O_VB), cols(_O_KA, _O_VA),
         cols(_O_VB, _O_ZB), cols(_O_VA, _O_ZA)], axis=1).T.astype(BF16)
    w_gates = jnp.concatenate(
        [cols(_O_ZA, _O_QB), cols(_O_ZB, _O_GA), cols(_O_GA, _O_END)], axis=1).astype(BF16)
    wts = (w_ada[0].astype(BF16), b_ada[0][None, :], norm_g[0][None, :], wT_att, w_gates,
           w_proj_a[0].astype(BF16), w_proj_b[0].astype(BF16), w_out[0].astype(BF16),
           subln_g[0][None, :], jnp.stack([lam_q1[0], lam_k1[0], lam_q2[0], lam_k2[0]], axis=0).astype(F32))
    bound_a = math.sqrt(HEAD_DIM) * jnp.max(jnp.abs(qn_a[0])) * jnp.max(jnp.abs(kn_a[0]))
    bound_b = math.sqrt(HEAD_DIM) * jnp.max(jnp.abs(qn_b[0])) * jnp.max(jnp.abs(kn_b[0]))
    safe = jnp.maximum(bound_a, bound_b) <= SAFE_SCORE_BOUND
    e_qa, e_ka = _fp8_scale_exponents(qn_a[0], kn_a[0], safe)
    e_qb, e_kb = _fp8_scale_exponents(qn_b[0], kn_b[0], safe)
    one = jnp.ones((), F32)
    fp8_scales = tuple(jnp.ldexp(one, e) for e in (e_qa, e_ka, e_qb, e_kb))
    unscales = (jnp.ldexp(one, -(e_qa + e_ka)).reshape(1), jnp.ldexp(one, -(e_qb + e_kb)).reshape(1))
    tables = _rope_coeff_tables(qn_a[0], kn_a[0], qn_b[0], kn_b[0], fp8_scales, SEQ)
    y_prompt = _layer(x_prompt, c_prompt, wts, tables, unscales, safe)
    y_sample = _layer(x_sample, c_sample, wts, tables, unscales, safe)
    return (y_prompt, y_sample)
```

```python
import functools
import math

import jax
import jax.numpy as jnp
from jax import lax
from jax.experimental import pallas as pl
from jax.experimental.pallas import tpu as pltpu

F32 = jnp.float32
BF16 = jnp.bfloat16

D_MODEL = 1024
SEQ = 8192
GRID_W = 64
HEAD_DIM = 64
HQ_A = 8
HKV_A = 2
GROUP_A = HQ_A // HKV_A
AX_DIM = HEAD_DIM // 2
AX_THETA = 10000.0
A_Q = HQ_A * HEAD_DIM
A_KV = HKV_A * HEAD_DIM
H_B = 4
B_QK = H_B * 2 * HEAD_DIM
B_V = H_B * 2 * HEAD_DIM
ROT_DIM = HEAD_DIM // 4
ROPE_THETA = 500000.0
EPS = 1e-6
LOG2E = math.log2(math.e)
Q_SCALE = LOG2E / math.sqrt(HEAD_DIM)
LAM_INIT = 0.8 - 0.6 * math.exp(-0.3 * 0)

_O_QA = 0
_O_KA = _O_QA + A_Q
_O_VA = _O_KA + A_KV
_O_ZA = _O_VA + A_KV
_O_QB = _O_ZA + A_Q
_O_KB = _O_QB + B_QK
_O_VB = _O_KB + B_QK
_O_ZB = _O_VB + B_V
_O_GA = _O_ZB + B_V
_O_GB = _O_GA + D_MODEL
_O_END = _O_GB + D_MODEL

_R_QA = 0
_R_QB = _R_QA + A_Q
_R_KB = _R_QB + B_QK
_R_KA = _R_KB + B_QK
_R_VB = _R_KA + A_KV
_R_VA = _R_VB + B_V
_R_END = _R_VA + A_KV

TS_IN = 512
IN_SUB = 256
IN_LOOKAHEAD = 1
TS_OUT = 1024
TQ = 512
Q_TILES_PER_STEP = 16
KV_SUB = 256
SCORE_LOOKAHEAD = 3
VMEM_LIMIT = 56 * 1024 * 1024
SAFE_SCORE_BOUND = 40.0
NEG_BIG = -1e30
assert TQ == TS_IN and TS_IN % IN_SUB == 0 and TS_IN % KV_SUB == 0


def _nt_dot(a, b):
    return lax.dot_general(a, b, (((1,), (1,)), ((), ())), preferred_element_type=F32)


def _sigmoid(x):
    return 1.0 / (1.0 + jnp.exp(-x))


def _modulated_norm(x, g, mod):
    ms = jnp.mean(x * x, axis=-1, keepdims=True)
    return (x * lax.rsqrt(ms + EPS)) * (g * (1.0 + mod[1:2])) + mod[0:1]


def _ada_kernel(c_ref, w_ref, b_ref, o_ref):
    c = c_ref[...]
    s = (c * _sigmoid(c)).astype(BF16)
    o_ref[...] = jnp.dot(s, w_ref[...], preferred_element_type=F32) + b_ref[...]


def _ada_mod(c, w_ada_bf, b_ada):
    nb = c.shape[0]
    return pl.pallas_call(
        _ada_kernel,
        out_shape=jax.ShapeDtypeStruct((nb, 3 * D_MODEL), F32),
        compiler_params=pltpu.CompilerParams(vmem_limit_bytes=VMEM_LIMIT),
        name="ada_mod",
    )(c, w_ada_bf, b_ada)


def _norm_rope_T(rows, n_heads, c_tab, s_tab, perm_slices):
    ts = rows.shape[-1]
    v = rows.reshape(n_heads, HEAD_DIM, ts)
    ms = jnp.mean(v * v, axis=1, keepdims=True)
    vh = v * lax.rsqrt(ms + EPS)
    partner = jnp.concatenate([vh[:, a:b] for a, b in perm_slices], axis=1)
    return vh * c_tab[None] + partner * s_tab[None]


_PERM_A = ((16, 32), (0, 16), (48, 64), (32, 48))
_PERM_B = ((8, 16), (0, 8), (16, 64))


def _inproj_kernel(x_ref, mod_ref, g_ref, wT_ref, tab_ref,
                   qaT_ref, vaT_ref, qbT_ref, vbT_ref, ka_ref, kb_ref):
    g = g_ref[...]
    mod = mod_ref[0]

    def finish(group, tok, pT):
        lanes = slice(tok, tok + IN_SUB)
        if group == "qa":
            r = _norm_rope_T(pT, HQ_A, tab_ref[0, :, lanes], tab_ref[1, :, lanes], _PERM_A)
            qaT_ref[0, :, 0, :, lanes] = r.astype(BF16)
        elif group == "qb":
            r = _norm_rope_T(pT, 2 * H_B, tab_ref[4, :, lanes], tab_ref[5, :, lanes], _PERM_B)
            qbT_ref[0, :, 0, :, lanes] = r.astype(BF16)
        elif group == "k":
            rb = _norm_rope_T(pT[:B_QK], 2 * H_B, tab_ref[6, :, lanes], tab_ref[7, :, lanes], _PERM_B)
            for hh in range(2 * H_B):
                kb_ref[0, hh, lanes, :] = rb[hh].T.astype(BF16)
            ra = _norm_rope_T(pT[B_QK:], HKV_A, tab_ref[2, :, lanes], tab_ref[3, :, lanes], _PERM_A)
            for hh in range(HKV_A):
                ka_ref[0, hh, lanes, :] = ra[hh].T.astype(BF16)
        else:
            vbT_ref[0, :, 0, :, lanes] = pT[:B_V].reshape(H_B, 2 * HEAD_DIM, IN_SUB).astype(BF16)
            vaT_ref[0, :, 0, :, lanes] = pT[B_V:].reshape(HKV_A, HEAD_DIM, IN_SUB).astype(BF16)

    n_sub = TS_IN // IN_SUB
    pending = []

    def finish_pair(first, tok, pT):
        if first:
            finish("qa", tok, pT[:A_Q])
            finish("qb", tok, pT[A_Q:])
        else:
            finish("k", tok, pT[:_R_VB - _R_KB])
            finish("v", tok, pT[_R_VB - _R_KB:])

    for sub in range(n_sub):
        tok = sub * IN_SUB
        h = _modulated_norm(x_ref[0, tok:tok + IN_SUB, :], g, mod).astype(BF16)
        for first, (lo, hi) in ((True, (_R_QA, _R_KB)), (False, (_R_KB, _R_END))):
            pending.append((first, tok, _nt_dot(wT_ref[lo:hi, :], h)))
            if len(pending) > IN_LOOKAHEAD:
                finish_pair(*pending.pop(0))
    while pending:
        finish_pair(*pending.pop(0))


def _inproj(x, mod3, norm_g, wT_att, tables):
    nb, s, d = x.shape
    nt = s // TS_IN
    out_shape = (
        jax.ShapeDtypeStruct((nb, HQ_A, nt, HEAD_DIM, TS_IN), BF16),
        jax.ShapeDtypeStruct((nb, HKV_A, nt, HEAD_DIM, TS_IN), BF16),
        jax.ShapeDtypeStruct((nb, 2 * H_B, nt, HEAD_DIM, TS_IN), BF16),
        jax.ShapeDtypeStruct((nb, H_B, nt, 2 * HEAD_DIM, TS_IN), BF16),
        jax.ShapeDtypeStruct((nb, HKV_A, s, HEAD_DIM), BF16),
        jax.ShapeDtypeStruct((nb, 2 * H_B, s, HEAD_DIM), BF16),
    )
    const = pl.Buffered(1)
    in_specs = [
        pl.BlockSpec((1, TS_IN, d), lambda t, b: (b, t, 0)),
        pl.BlockSpec((1, 3, d), lambda t, b: (b, 0, 0)),
        pl.BlockSpec((1, d), lambda t, b: (0, 0), pipeline_mode=const),
        pl.BlockSpec((_R_END, d), lambda t, b: (0, 0), pipeline_mode=const),
        pl.BlockSpec((8, HEAD_DIM, TS_IN), lambda t, b: (0, 0, t)),
    ]
    out_specs = (
        pl.BlockSpec((1, HQ_A, 1, HEAD_DIM, TS_IN), lambda t, b: (b, 0, t, 0, 0)),
        pl.BlockSpec((1, HKV_A, 1, HEAD_DIM, TS_IN), lambda t, b: (b, 0, t, 0, 0)),
        pl.BlockSpec((1, 2 * H_B, 1, HEAD_DIM, TS_IN), lambda t, b: (b, 0, t, 0, 0)),
        pl.BlockSpec((1, H_B, 1, 2 * HEAD_DIM, TS_IN), lambda t, b: (b, 0, t, 0, 0)),
        pl.BlockSpec((1, HKV_A, TS_IN, HEAD_DIM), lambda t, b: (b, 0, t, 0)),
        pl.BlockSpec((1, 2 * H_B, TS_IN, HEAD_DIM), lambda t, b: (b, 0, t, 0)),
    )
    return pl.pallas_call(
        _inproj_kernel,
        grid=(nt, nb),
        in_specs=in_specs,
        out_specs=out_specs,
        out_shape=out_shape,
        compiler_params=pltpu.CompilerParams(
            dimension_semantics=("arbitrary", "arbitrary"), vmem_limit_bytes=VMEM_LIMIT),
        name="in_proj",
    )(x, mod3, norm_g, wT_att, tables)


def _attend(n_maps, n_q, q_of, k_of, v_of, combine, emit, acc_ref, m_ref, l_ref, fin_ref, *, online):
    tq = acc_ref.shape[-1]
    n_chunks = k_of(0).shape[0] // TS_IN
    subs = TS_IN // KV_SUB

    def scores(t, i, c, w):
        off = c * TS_IN + w * KV_SUB
        if not isinstance(off, int):
            off = pl.multiple_of(off, KV_SUB)
        return jnp.dot(k_of(i)[pl.ds(off, KV_SUB), :], q_of(i, t), preferred_element_type=F32)

    def normalised(acc, l8):
        return acc * (1.0 / jnp.sum(l8, axis=0, keepdims=True))

    if online:
        def online_tile(t, carry):
            acc_ref[...] = jnp.zeros(acc_ref.shape, F32)
            l_ref[...] = jnp.zeros(l_ref.shape, F32)
            m_ref[...] = jnp.full(m_ref.shape, NEG_BIG, F32)

            def online_chunk(c, inner):
                for w in range(subs):
                    for i in range(n_maps):
                        s = scores(t, i, c, w)
                        m_prev = m_ref[i]
                        m_new = jnp.maximum(m_prev, jnp.max(s, axis=0, keepdims=True))
                        alpha = jnp.exp2(m_prev - m_new)
                        p = jnp.exp2(s - m_new)
                        l_ref[i] = alpha * l_ref[i] + p.reshape(KV_SUB // 8, 8, tq).sum(axis=0)
                        pv = jnp.dot(v_of(c, w), p.astype(BF16), preferred_element_type=F32)
                        acc_ref[i] = alpha * acc_ref[i] + pv
                        m_ref[i] = m_new
                return inner

            lax.fori_loop(0, n_chunks, online_chunk, 0)
            emit(t, combine([normalised(acc_ref[i], l_ref[i]) for i in range(n_maps)]))
            return carry

        lax.fori_loop(0, n_q, online_tile, 0)
        return

    items = [(i, c, w) for c in range(n_chunks) for w in range(subs) for i in range(n_maps)]
    fin_ref[...] = jnp.zeros(fin_ref.shape, F32)

    def raw_tile(t, carry):
        emit(jnp.maximum(t - 1, 0), fin_ref[...])
        l_part = [None] * n_maps
        pv_part = [None] * n_maps
        pending = []

        def consume(item, s):
            i, c, w = item
            p = jnp.exp2(s)
            lp = p.reshape(KV_SUB // 8, 8, tq).sum(axis=0)
            pv = jnp.dot(v_of(c, w), p.astype(BF16), preferred_element_type=F32)
            l_part[i] = lp if l_part[i] is None else l_part[i] + lp
            pv_part[i] = pv if pv_part[i] is None else pv_part[i] + pv

        for item in items:
            pending.append((item, scores(t, *item)))
            if len(pending) > SCORE_LOOKAHEAD:
                consume(*pending.pop(0))
        while pending:
            consume(*pending.pop(0))
        fin_ref[...] = combine([normalised(pv_part[i], l_part[i]) for i in range(n_maps)])
        return carry

    lax.fori_loop(0, n_q, raw_tile, 0)
    emit(n_q - 1, fin_ref[...])


def _attn_a_kernel(qT_ref, k_ref, vT_ref, o_ref, acc_ref, m_ref, l_ref, fin_ref, *, online):
    def emit(t, oT):
        rows = pl.ds(pl.multiple_of(t * TQ, TQ), TQ)
        o_ref[0, rows, :] = oT.T

    _attend(GROUP_A, qT_ref.shape[2], lambda i, t: qT_ref[0, i, t], lambda i: k_ref.at[0, 0],
            lambda c, w: vT_ref[0, 0, c, :, w * KV_SUB:(w + 1) * KV_SUB],
            lambda outs: jnp.concatenate(outs, axis=0), emit,
            acc_ref, m_ref, l_ref, fin_ref, online=online)


def _attn_b_kernel(lam_ref, qT_ref, k_ref, vT_ref, o_ref, acc_ref, m_ref, l_ref, fin_ref, *, online):
    lv = lam_ref[...]
    lam = (jnp.exp(jnp.sum(lv[0:1] * lv[1:2], axis=1, keepdims=True))
           - jnp.exp(jnp.sum(lv[2:3] * lv[3:4], axis=1, keepdims=True)) + LAM_INIT)

    def emit(t, oT):
        rows = pl.ds(pl.multiple_of(t * TQ, TQ), TQ)
        o_ref[0, rows, :] = oT.T

    _attend(2, qT_ref.shape[2], lambda i, t: qT_ref[0, i, t], lambda i: k_ref.at[0, i],
            lambda c, w: vT_ref[0, 0, c, :, w * KV_SUB:(w + 1) * KV_SUB],
            lambda outs: outs[0] - lam * outs[1], emit,
            acc_ref, m_ref, l_ref, fin_ref, online=online)


def _attn_a(qaT, ka, vaT, *, online):
    nb, _, nq, _, _ = qaT.shape
    s = nq * TQ
    nc = vaT.shape[2]
    width = GROUP_A * HEAD_DIM
    return pl.pallas_call(
        functools.partial(_attn_a_kernel, online=online),
        grid=(nb, HKV_A, nq // Q_TILES_PER_STEP),
        in_specs=[
            pl.BlockSpec((1, GROUP_A, Q_TILES_PER_STEP, HEAD_DIM, TQ), lambda b, g, i: (b, g, i, 0, 0)),
            pl.BlockSpec((1, 1, s, HEAD_DIM), lambda b, g, i: (b, g, 0, 0)),
            pl.BlockSpec((1, 1, nc, HEAD_DIM, TS_IN), lambda b, g, i: (b, g, 0, 0, 0)),
        ],
        out_specs=pl.BlockSpec((1, Q_TILES_PER_STEP * TQ, width), lambda b, g, i: (b, i, g)),
        out_shape=jax.ShapeDtypeStruct((nb, s, A_Q), F32),
        scratch_shapes=[
            pltpu.VMEM((GROUP_A, HEAD_DIM, TQ), F32),
            pltpu.VMEM((GROUP_A, 1, TQ), F32),
            pltpu.VMEM((GROUP_A, 8, TQ), F32),
            pltpu.VMEM((width, TQ), F32),
        ],
        compiler_params=pltpu.CompilerParams(
            dimension_semantics=("arbitrary", "arbitrary", "arbitrary"), vmem_limit_bytes=VMEM_LIMIT),
        name="attn_gqa_online" if online else "attn_gqa",
    )(qaT, ka, vaT)


def _attn_b(lam_vecs, qbT, kb, vbT, *, online):
    nb, _, nq, _, _ = qbT.shape
    s = nq * TQ
    nc = vbT.shape[2]
    dv = 2 * HEAD_DIM
    return pl.pallas_call(
        functools.partial(_attn_b_kernel, online=online),
        grid=(nb, H_B, nq // Q_TILES_PER_STEP),
        in_specs=[
            pl.BlockSpec((4, HEAD_DIM), lambda b, h, i: (0, 0)),
            pl.BlockSpec((1, 2, Q_TILES_PER_STEP, HEAD_DIM, TQ), lambda b, h, i: (b, h, i, 0, 0)),
            pl.BlockSpec((1, 2, s, HEAD_DIM), lambda b, h, i: (b, h, 0, 0)),
            pl.BlockSpec((1, 1, nc, dv, TS_IN), lambda b, h, i: (b, h, 0, 0, 0)),
        ],
        out_specs=pl.BlockSpec((1, Q_TILES_PER_STEP * TQ, dv), lambda b, h, i: (b, i, h)),
        out_shape=jax.ShapeDtypeStruct((nb, s, B_V), F32),
        scratch_shapes=[
            pltpu.VMEM((2, dv, TQ), F32),
            pltpu.VMEM((2, 1, TQ), F32),
            pltpu.VMEM((2, 8, TQ), F32),
            pltpu.VMEM((dv, TQ), F32),
        ],
        compiler_params=pltpu.CompilerParams(
            dimension_semantics=("arbitrary", "arbitrary", "arbitrary"), vmem_limit_bytes=VMEM_LIMIT),
        name="attn_diff_online" if online else "attn_diff",
    )(lam_vecs, qbT, kb, vbT)


def _out_kernel(x_ref, mod_ref, g_ref, oa_ref, ob_ref, wg_ref, wpa_ref, wpb_ref, wo_ref, sg_ref, y_ref):
    x = x_ref[0]
    mod = mod_ref[0]
    h = _modulated_norm(x, g_ref[...], mod).astype(BF16)
    gates = jnp.dot(h, wg_ref[...], preferred_element_type=F32)
    za = gates[:, 0:A_Q]
    zb = gates[:, A_Q:A_Q + B_V]
    ga = gates[:, A_Q + B_V:A_Q + B_V + D_MODEL]
    gb = gates[:, A_Q + B_V + D_MODEL:]
    a = oa_ref[0] * (za * _sigmoid(za))
    pa = jnp.dot(a.astype(BF16), wpa_ref[...], preferred_element_type=F32)
    dv = 2 * HEAD_DIM
    ob = ob_ref[0]
    sg = sg_ref[...] * (1.0 - LAM_INIT)
    normed = []
    for hh in range(H_B):
        oh = ob[:, hh * dv:(hh + 1) * dv]
        ms = jnp.mean(oh * oh, axis=-1, keepdims=True)
        normed.append(oh * lax.rsqrt(ms + EPS) * sg)
    bn = jnp.concatenate(normed, axis=-1) * (zb * _sigmoid(zb))
    pb = jnp.dot(bn.astype(BF16), wpb_ref[...], preferred_element_type=F32)
    merged = _sigmoid(ga) * pa + _sigmoid(gb) * pb
    out = jnp.dot(merged.astype(BF16), wo_ref[...], preferred_element_type=F32)
    y_ref[0] = x + mod[2:3] * out


def _out_stage(x, mod3, norm_g, oa, ob, w_gates, w_pa, w_pb, w_out, subln_g):
    nb, s, d = x.shape
    const = pl.Buffered(1)
    n_gate = w_gates.shape[1]
    return pl.pallas_call(
        _out_kernel,
        grid=(nb, s // TS_OUT),
        in_specs=[
            pl.BlockSpec((1, TS_OUT, d), lambda b, t: (b, t, 0)),
            pl.BlockSpec((1, 3, d), lambda b, t: (b, 0, 0)),
            pl.BlockSpec((1, d), lambda b, t: (0, 0), pipeline_mode=const),
            pl.BlockSpec((1, TS_OUT, A_Q), lambda b, t: (b, t, 0)),
            pl.BlockSpec((1, TS_OUT, B_V), lambda b, t: (b, t, 0)),
            pl.BlockSpec((d, n_gate), lambda b, t: (0, 0), pipeline_mode=const),
            pl.BlockSpec((A_Q, d), lambda b, t: (0, 0), pipeline_mode=const),
            pl.BlockSpec((B_V, d), lambda b, t: (0, 0), pipeline_mode=const),
            pl.BlockSpec((d, d), lambda b, t: (0, 0), pipeline_mode=const),
            pl.BlockSpec((1, 2 * HEAD_DIM), lambda b, t: (0, 0), pipeline_mode=const),
        ],
        out_specs=pl.BlockSpec((1, TS_OUT, d), lambda b, t: (b, t, 0)),
        out_shape=jax.ShapeDtypeStruct((nb, s, d), F32),
        compiler_params=pltpu.CompilerParams(
            dimension_semantics=("arbitrary", "arbitrary"), vmem_limit_bytes=VMEM_LIMIT),
        name="out_stage",
    )(x, mod3, norm_g, oa, ob, w_gates, w_pa, w_pb, w_out, subln_g)


def _rope_coeff_tables(qn_a, kn_a, qn_b, kn_b, s):
    t = jnp.arange(s)
    row = (t // GRID_W).astype(F32)
    col = (t % GRID_W).astype(F32)
    inv_ax = AX_THETA ** (-jnp.arange(0, AX_DIM, 2, dtype=F32) / AX_DIM)
    ang_r = inv_ax[:, None] * row[None, :]
    ang_c = inv_ax[:, None] * col[None, :]
    ang_a = jnp.concatenate([ang_r, ang_r, ang_c, ang_c], axis=0)
    cos_a, sin_a = jnp.cos(ang_a), jnp.sin(ang_a)
    half = AX_DIM // 2
    sign_a = jnp.concatenate([-jnp.ones(half), jnp.ones(half), -jnp.ones(half), jnp.ones(half)]).astype(F32)
    perm_a = jnp.concatenate([jnp.arange(a, b) for a, b in _PERM_A])

    inv_p = ROPE_THETA ** (-jnp.arange(0, ROT_DIM, 2, dtype=F32) / ROT_DIM)
    ang_p = inv_p[:, None] * t.astype(F32)[None, :]
    ang_p = jnp.concatenate([ang_p, ang_p], axis=0)
    rest = HEAD_DIM - ROT_DIM
    cos_b = jnp.concatenate([jnp.cos(ang_p), jnp.ones((rest, s), F32)], axis=0)
    sin_b = jnp.concatenate([jnp.sin(ang_p), jnp.zeros((rest, s), F32)], axis=0)
    hb = ROT_DIM // 2
    sign_b = jnp.concatenate([-jnp.ones(hb), jnp.ones(hb), jnp.zeros(rest)]).astype(F32)
    perm_b = jnp.concatenate([jnp.arange(a, b) for a, b in _PERM_B])

    def pair(g, cos, sin, sign, perm, scale):
        g = g.astype(F32)
        return [scale * g[:, None] * cos, scale * (sign * g[perm])[:, None] * sin]

    tabs = (pair(qn_a, cos_a, sin_a, sign_a, perm_a, Q_SCALE) + pair(kn_a, cos_a, sin_a, sign_a, perm_a, 1.0)
            + pair(qn_b, cos_b, sin_b, sign_b, perm_b, Q_SCALE) + pair(kn_b, cos_b, sin_b, sign_b, perm_b, 1.0))
    return jnp.stack(tabs, axis=0)


def _layer(x, c, wts, tables, safe):
    (w_ada_bf, b_ada, norm_g, wT_att, w_gates, w_pa, w_pb, w_out, subln_g, lam_vecs) = wts
    nb = x.shape[0]
    mod3 = _ada_mod(c, w_ada_bf, b_ada).reshape(nb, 3, D_MODEL)
    qaT, vaT, qbT, vbT, ka, kb = _inproj(x, mod3, norm_g, wT_att, tables)

    def attend(online):
        def run(ops):
            qaT_, ka_, vaT_, qbT_, kb_, vbT_ = ops
            return (_attn_a(qaT_, ka_, vaT_, online=online),
                    _attn_b(lam_vecs, qbT_, kb_, vbT_, online=online))
        return run

    oa, ob = lax.cond(safe, attend(False), attend(True), (qaT, ka, vaT, qbT, kb, vbT))
    return _out_stage(x, mod3, norm_g, oa, ob, w_gates, w_pa, w_pb, w_out, subln_g)


def kernel(x_prompt, x_sample, c_prompt, c_sample, w_ada, b_ada, norm_g, w_in, qn_a, kn_a, qn_b, kn_b,
           lam_q1, lam_k1, lam_q2, lam_k2, subln_g, w_proj_a, w_proj_b, w_out):
    w = w_in[0]
    cols = lambda a, b: w[:, a:b]
    wT_att = jnp.concatenate(
        [cols(_O_QA, _O_KA), cols(_O_QB, _O_KB), cols(_O_KB, _O_VB), cols(_O_KA, _O_VA),
         cols(_O_VB, _O_ZB), cols(_O_VA, _O_ZA)], axis=1).T.astype(BF16)
    w_gates = jnp.concatenate(
        [cols(_O_ZA, _O_QB), cols(_O_ZB, _O_GA), cols(_O_GA, _O_END)], axis=1).astype(BF16)
    wts = (w_ada[0].astype(BF16), b_ada[0][None, :], norm_g[0][None, :], wT_att, w_gates,
           w_proj_a[0].astype(BF16), w_proj_b[0].astype(BF16), w_out[0].astype(BF16),
           subln_g[0][None, :], jnp.stack([lam_q1[0], lam_k1[0], lam_q2[0], lam_k2[0]], axis=0).astype(F32))
    tables = _rope_coeff_tables(qn_a[0], kn_a[0], qn_b[0], kn_b[0], SEQ)
    bound_a = math.sqrt(HEAD_DIM) * jnp.max(jnp.abs(qn_a[0])) * jnp.max(jnp.abs(kn_a[0]))
    bound_b = math.sqrt(HEAD_DIM) * jnp.max(jnp.abs(qn_b[0])) * jnp.max(jnp.abs(kn_b[0]))
    safe = jnp.maximum(bound_a, bound_b) <= SAFE_SCORE_BOUND
    y_prompt = _layer(x_prompt, c_prompt, wts, tables, safe)
    y_sample = _layer(x_sample, c_sample, wts, tables, safe)
    return (y_prompt, y_sample)
```

```python
import functools
import math

import jax
import jax.numpy as jnp
from jax import lax
from jax.experimental import pallas as pl
from jax.experimental.pallas import tpu as pltpu

F32 = jnp.float32
BF16 = jnp.bfloat16

D_MODEL = 1024
SEQ = 8192
GRID_W = 64
HEAD_DIM = 64
HQ_A = 8
HKV_A = 2
GROUP_A = HQ_A // HKV_A
AX_DIM = HEAD_DIM // 2
AX_THETA = 10000.0
A_Q = HQ_A * HEAD_DIM
A_KV = HKV_A * HEAD_DIM
H_B = 4
B_QK = H_B * 2 * HEAD_DIM
B_V = H_B * 2 * HEAD_DIM
ROT_DIM = HEAD_DIM // 4
ROPE_THETA = 500000.0
EPS = 1e-6
LOG2E = math.log2(math.e)
Q_SCALE = LOG2E / math.sqrt(HEAD_DIM)
LAM_INIT = 0.8 - 0.6 * math.exp(-0.3 * 0)

_O_QA = 0
_O_KA = _O_QA + A_Q
_O_VA = _O_KA + A_KV
_O_ZA = _O_VA + A_KV
_O_QB = _O_ZA + A_Q
_O_KB = _O_QB + B_QK
_O_VB = _O_KB + B_QK
_O_ZB = _O_VB + B_V
_O_GA = _O_ZB + B_V
_O_GB = _O_GA + D_MODEL
_O_END = _O_GB + D_MODEL

_R_QA = 0
_R_QB = _R_QA + A_Q
_R_KB = _R_QB + B_QK
_R_KA = _R_KB + B_QK
_R_VB = _R_KA + A_KV
_R_VA = _R_VB + B_V
_R_END = _R_VA + A_KV

TS_IN = 512
IN_SUB = 256
IN_LOOKAHEAD = 1
TS_OUT = 1024
TQ = 512
Q_TILES_PER_STEP = 16
KV_SUB = 256
SCORE_LOOKAHEAD = 3
VMEM_LIMIT = 56 * 1024 * 1024
SAFE_SCORE_BOUND = 40.0
NEG_BIG = -1e30
assert TQ == TS_IN and TS_IN % IN_SUB == 0 and TS_IN % KV_SUB == 0


def _nt_dot(a, b):
    return lax.dot_general(a, b, (((1,), (1,)), ((), ())), preferred_element_type=F32)


def _sigmoid(x):
    return 1.0 / (1.0 + jnp.exp(-x))


def _modulated_norm(x, g, mod):
    ms = jnp.mean(x * x, axis=-1, keepdims=True)
    return (x * lax.rsqrt(ms + EPS)) * (g * (1.0 + mod[1:2])) + mod[0:1]


def _ada_kernel(c_ref, w_ref, b_ref, o_ref):
    c = c_ref[...]
    s = (c * _sigmoid(c)).astype(BF16)
    o_ref[...] = jnp.dot(s, w_ref[...], preferred_element_type=F32) + b_ref[...]


def _ada_mod(c, w_ada_bf, b_ada):
    nb = c.shape[0]
    return pl.pallas_call(
        _ada_kernel,
        out_shape=jax.ShapeDtypeStruct((nb, 3 * D_MODEL), F32),
        compiler_params=pltpu.CompilerParams(vmem_limit_bytes=VMEM_LIMIT),
        name="ada_mod",
    )(c, w_ada_bf, b_ada)


def _norm_rope_T(rows, n_heads, c_tab, s_tab, perm_slices):
    ts = rows.shape[-1]
    v = rows.reshape(n_heads, HEAD_DIM, ts)
    ms = jnp.mean(v * v, axis=1, keepdims=True)
    vh = v * lax.rsqrt(ms + EPS)
    partner = jnp.concatenate([vh[:, a:b] for a, b in perm_slices], axis=1)
    return vh * c_tab[None] + partner * s_tab[None]


_PERM_A = ((16, 32), (0, 16), (48, 64), (32, 48))
_PERM_B = ((8, 16), (0, 8), (16, 64))


def _inproj_kernel(x_ref, mod_ref, g_ref, wT_ref, tab_ref,
                   qaT_ref, vaT_ref, qbT_ref, vbT_ref, ka_ref, kb_ref):
    g = g_ref[...]
    mod = mod_ref[0]

    def finish(group, tok, pT):
        lanes = slice(tok, tok + IN_SUB)
        if group == "qa":
            r = _norm_rope_T(pT, HQ_A, tab_ref[0, :, lanes], tab_ref[1, :, lanes], _PERM_A)
            qaT_ref[0, :, 0, :, lanes] = r.astype(BF16)
        elif group == "qb":
            r = _norm_rope_T(pT, 2 * H_B, tab_ref[4, :, lanes], tab_ref[5, :, lanes], _PERM_B)
            qbT_ref[0, :, 0, :, lanes] = r.astype(BF16)
        elif group == "k":
            rb = _norm_rope_T(pT[:B_QK], 2 * H_B, tab_ref[6, :, lanes], tab_ref[7, :, lanes], _PERM_B)
            for hh in range(2 * H_B):
                kb_ref[0, hh, lanes, :] = rb[hh].T.astype(BF16)
            ra = _norm_rope_T(pT[B_QK:], HKV_A, tab_ref[2, :, lanes], tab_ref[3, :, lanes], _PERM_A)
            for hh in range(HKV_A):
                ka_ref[0, hh, lanes, :] = ra[hh].T.astype(BF16)
        else:
            vbT_ref[0, :, 0, :, lanes] = pT[:B_V].reshape(H_B, 2 * HEAD_DIM, IN_SUB).astype(BF16)
            vaT_ref[0, :, 0, :, lanes] = pT[B_V:].reshape(HKV_A, HEAD_DIM, IN_SUB).astype(BF16)

    rows = {"qa": (_R_QA, _R_QB), "qb": (_R_QB, _R_KB), "k": (_R_KB, _R_VB), "v": (_R_VB, _R_END)}
    n_sub = TS_IN // IN_SUB
    pending = []
    for sub in range(n_sub):
        tok = sub * IN_SUB
        h = _modulated_norm(x_ref[0, tok:tok + IN_SUB, :], g, mod).astype(BF16)
        order = ("k", "qa", "qb", "v") if sub == n_sub - 1 else ("qa", "v", "qb", "k")
        for name in order:
            lo, hi = rows[name]
            pending.append((name, tok, _nt_dot(wT_ref[lo:hi, :], h)))
            if len(pending) > IN_LOOKAHEAD:
                finish(*pending.pop(0))
    while pending:
        finish(*pending.pop(0))


def _inproj(x, mod3, norm_g, wT_att, tables):
    nb, s, d = x.shape
    nt = s // TS_IN
    out_shape = (
        jax.ShapeDtypeStruct((nb, HQ_A, nt, HEAD_DIM, TS_IN), BF16),
        jax.ShapeDtypeStruct((nb, HKV_A, nt, HEAD_DIM, TS_IN), BF16),
        jax.ShapeDtypeStruct((nb, 2 * H_B, nt, HEAD_DIM, TS_IN), BF16),
        jax.ShapeDtypeStruct((nb, H_B, nt, 2 * HEAD_DIM, TS_IN), BF16),
        jax.ShapeDtypeStruct((nb, HKV_A, s, HEAD_DIM), BF16),
        jax.ShapeDtypeStruct((nb, 2 * H_B, s, HEAD_DIM), BF16),
    )
    const = pl.Buffered(1)
    in_specs = [
        pl.BlockSpec((1, TS_IN, d), lambda t, b: (b, t, 0)),
        pl.BlockSpec((1, 3, d), lambda t, b: (b, 0, 0)),
        pl.BlockSpec((1, d), lambda t, b: (0, 0), pipeline_mode=const),
        pl.BlockSpec((_R_END, d), lambda t, b: (0, 0), pipeline_mode=const),
        pl.BlockSpec((8, HEAD_DIM, TS_IN), lambda t, b: (0, 0, t)),
    ]
    out_specs = (
        pl.BlockSpec((1, HQ_A, 1, HEAD_DIM, TS_IN), lambda t, b: (b, 0, t, 0, 0)),
        pl.BlockSpec((1, HKV_A, 1, HEAD_DIM, TS_IN), lambda t, b: (b, 0, t, 0, 0)),
        pl.BlockSpec((1, 2 * H_B, 1, HEAD_DIM, TS_IN), lambda t, b: (b, 0, t, 0, 0)),
        pl.BlockSpec((1, H_B, 1, 2 * HEAD_DIM, TS_IN), lambda t, b: (b, 0, t, 0, 0)),
        pl.BlockSpec((1, HKV_A, TS_IN, HEAD_DIM), lambda t, b: (b, 0, t, 0)),
        pl.BlockSpec((1, 2 * H_B, TS_IN, HEAD_DIM), lambda t, b: (b, 0, t, 0)),
    )
    return pl.pallas_call(
        _inproj_kernel,
        grid=(nt, nb),
        in_specs=in_specs,
        out_specs=out_specs,
        out_shape=out_shape,
        compiler_params=pltpu.CompilerParams(
            dimension_semantics=("arbitrary", "arbitrary"), vmem_limit_bytes=VMEM_LIMIT),
        name="in_proj",
    )(x, mod3, norm_g, wT_att, tables)


def _attend(n_maps, n_q, q_of, k_of, v_of, combine, emit, acc_ref, m_ref, l_ref, fin_ref, *, online):
    tq = acc_ref.shape[-1]
    n_chunks = k_of(0).shape[0] // TS_IN
    subs = TS_IN // KV_SUB

    def scores(t, i, c, w):
        off = c * TS_IN + w * KV_SUB
        if not isinstance(off, int):
            off = pl.multiple_of(off, KV_SUB)
        return jnp.dot(k_of(i)[pl.ds(off, KV_SUB), :], q_of(i, t), preferred_element_type=F32)

    def normalised(acc, l8):
        return acc * (1.0 / jnp.sum(l8, axis=0, keepdims=True))

    if online:
        def online_tile(t, carry):
            acc_ref[...] = jnp.zeros(acc_ref.shape, F32)
            l_ref[...] = jnp.zeros(l_ref.shape, F32)
            m_ref[...] = jnp.full(m_ref.shape, NEG_BIG, F32)

            def online_chunk(c, inner):
                for w in range(subs):
                    for i in range(n_maps):
                        s = scores(t, i, c, w)
                        m_prev = m_ref[i]
                        m_new = jnp.maximum(m_prev, jnp.max(s, axis=0, keepdims=True))
                        alpha = jnp.exp2(m_prev - m_new)
                        p = jnp.exp2(s - m_new)
                        l_ref[i] = alpha * l_ref[i] + p.reshape(KV_SUB // 8, 8, tq).sum(axis=0)
                        pv = jnp.dot(v_of(c, w), p.astype(BF16), preferred_element_type=F32)
                        acc_ref[i] = alpha * acc_ref[i] + pv
                        m_ref[i] = m_new
                return inner

            lax.fori_loop(0, n_chunks, online_chunk, 0)
            emit(t, combine([normalised(acc_ref[i], l_ref[i]) for i in range(n_maps)]))
            return carry

        lax.fori_loop(0, n_q, online_tile, 0)
        return

    items = [(i, c, w) for c in range(n_chunks) for w in range(subs) for i in range(n_maps)]
    fin_ref[...] = jnp.zeros(fin_ref.shape, F32)

    def raw_tile(t, carry):
        emit(jnp.maximum(t - 1, 0), fin_ref[...])
        l_part = [None] * n_maps
        pv_part = [None] * n_maps
        pending = []

        def consume(item, s):
            i, c, w = item
            p = jnp.exp2(s)
            lp = p.reshape(KV_SUB // 8, 8, tq).sum(axis=0)
            pv = jnp.dot(v_of(c, w), p.astype(BF16), preferred_element_type=F32)
            l_part[i] = lp if l_part[i] is None else l_part[i] + lp
            pv_part[i] = pv if pv_part[i] is None else pv_part[i] + pv

        for item in items:
            pending.append((item, scores(t, *item)))
            if len(pending) > SCORE_LOOKAHEAD:
                consume(*pending.pop(0))
        while pending:
            consume(*pending.pop(0))
        fin_ref[...] = combine([normalised(pv_part[i], l_part[i]) for i in range(n_maps)])
        return carry

    lax.fori_loop(0, n_q, raw_tile, 0)
    emit(n_q - 1, fin_ref[...])


def _attn_a_kernel(qT_ref, k_ref, vT_ref, o_ref, acc_ref, m_ref, l_ref, fin_ref, *, online):
    def emit(t, oT):
        rows = pl.ds(pl.multiple_of(t * TQ, TQ), TQ)
        o_ref[0, rows, :] = oT.T

    _attend(GROUP_A, qT_ref.shape[2], lambda i, t: qT_ref[0, i, t], lambda i: k_ref.at[0, 0],
            lambda c, w: vT_ref[0, 0, c, :, w * KV_SUB:(w + 1) * KV_SUB],
            lambda outs: jnp.concatenate(outs, axis=0), emit,
            acc_ref, m_ref, l_ref, fin_ref, online=online)


def _attn_b_kernel(lam_ref, qT_ref, k_ref, vT_ref, o_ref, acc_ref, m_ref, l_ref, fin_ref, *, online):
    lv = lam_ref[...]
    lam = (jnp.exp(jnp.sum(lv[0:1] * lv[1:2], axis=1, keepdims=True))
           - jnp.exp(jnp.sum(lv[2:3] * lv[3:4], axis=1, keepdims=True)) + LAM_INIT)

    def emit(t, oT):
        rows = pl.ds(pl.multiple_of(t * TQ, TQ), TQ)
        o_ref[0, rows, :] = oT.T

    _attend(2, qT_ref.shape[2], lambda i, t: qT_ref[0, i, t], lambda i: k_ref.at[0, i],
            lambda c, w: vT_ref[0, 0, c, :, w * KV_SUB:(w + 1) * KV_SUB],
            lambda outs: outs[0] - lam * outs[1], emit,
            acc_ref, m_ref, l_ref, fin_ref, online=online)


def _attn_a(qaT, ka, vaT, *, online):
    nb, _, nq, _, _ = qaT.shape
    s = nq * TQ
    nc = vaT.shape[2]
    width = GROUP_A * HEAD_DIM
    return pl.pallas_call(
        functools.partial(_attn_a_kernel, online=online),
        grid=(nb, HKV_A, nq // Q_TILES_PER_STEP),
        in_specs=[
            pl.BlockSpec((1, GROUP_A, Q_TILES_PER_STEP, HEAD_DIM, TQ), lambda b, g, i: (b, g, i, 0, 0)),
            pl.BlockSpec((1, 1, s, HEAD_DIM), lambda b, g, i: (b, g, 0, 0)),
            pl.BlockSpec((1, 1, nc, HEAD_DIM, TS_IN), lambda b, g, i: (b, g, 0, 0, 0)),
        ],
        out_specs=pl.BlockSpec((1, Q_TILES_PER_STEP * TQ, width), lambda b, g, i: (b, i, g)),
        out_shape=jax.ShapeDtypeStruct((nb, s, A_Q), F32),
        scratch_shapes=[
            pltpu.VMEM((GROUP_A, HEAD_DIM, TQ), F32),
            pltpu.VMEM((GROUP_A, 1, TQ), F32),
            pltpu.VMEM((GROUP_A, 8, TQ), F32),
            pltpu.VMEM((width, TQ), F32),
        ],
        compiler_params=pltpu.CompilerParams(
            dimension_semantics=("arbitrary", "arbitrary", "arbitrary"), vmem_limit_bytes=VMEM_LIMIT),
        name="attn_gqa_online" if online else "attn_gqa",
    )(qaT, ka, vaT)


def _attn_b(lam_vecs, qbT, kb, vbT, *, online):
    nb, _, nq, _, _ = qbT.shape
    s = nq * TQ
    nc = vbT.shape[2]
    dv = 2 * HEAD_DIM
    return pl.pallas_call(
        functools.partial(_attn_b_kernel, online=online),
        grid=(nb, H_B, nq // Q_TILES_PER_STEP),
        in_specs=[
            pl.BlockSpec((4, HEAD_DIM), lambda b, h, i: (0, 0)),
            pl.BlockSpec((1, 2, Q_TILES_PER_STEP, HEAD_DIM, TQ), lambda b, h, i: (b, h, i, 0, 0)),
            pl.BlockSpec((1, 2, s, HEAD_DIM), lambda b, h, i: (b, h, 0, 0)),
            pl.BlockSpec((1, 1, nc, dv, TS_IN), lambda b, h, i: (b, h, 0, 0, 0)),
        ],
        out_specs=pl.BlockSpec((1, Q_TILES_PER_STEP * TQ, dv), lambda b, h, i: (b, i, h)),
        out_shape=jax.ShapeDtypeStruct((nb, s, B_V), F32),
        scratch_shapes=[
            pltpu.VMEM((2, dv, TQ), F32),
            pltpu.VMEM((2, 1, TQ), F32),
            pltpu.VMEM((2, 8, TQ), F32),
            pltpu.VMEM((dv, TQ), F32),
        ],
        compiler_params=pltpu.CompilerParams(
            dimension_semantics=("arbitrary", "arbitrary", "arbitrary"), vmem_limit_bytes=VMEM_LIMIT),
        name="attn_diff_online" if online else "attn_diff",
    )(lam_vecs, qbT, kb, vbT)


def _out_kernel(x_ref, mod_ref, g_ref, oa_ref, ob_ref, wg_ref, wpa_ref, wpb_ref, wo_ref, sg_ref, y_ref):
    x = x_ref[0]
    mod = mod_ref[0]
    h = _modulated_norm(x, g_ref[...], mod).astype(BF16)
    gates = jnp.dot(h, wg_ref[...], preferred_element_type=F32)
    za = gates[:, 0:A_Q]
    zb = gates[:, A_Q:A_Q + B_V]
    ga = gates[:, A_Q + B_V:A_Q + B_V + D_MODEL]
    gb = gates[:, A_Q + B_V + D_MODEL:]
    a = oa_ref[0] * (za * _sigmoid(za))
    pa = jnp.dot(a.astype(BF16), wpa_ref[...], preferred_element_type=F32)
    dv = 2 * HEAD_DIM
    ob = ob_ref[0]
    sg = sg_ref[...] * (1.0 - LAM_INIT)
    normed = []
    for hh in range(H_B):
        oh = ob[:, hh * dv:(hh + 1) * dv]
        ms = jnp.mean(oh * oh, axis=-1, keepdims=True)
        normed.append(oh * lax.rsqrt(ms + EPS) * sg)
    bn = jnp.concatenate(normed, axis=-1) * (zb * _sigmoid(zb))
    pb = jnp.dot(bn.astype(BF16), wpb_ref[...], preferred_element_type=F32)
    merged = _sigmoid(ga) * pa + _sigmoid(gb) * pb
    out = jnp.dot(merged.astype(BF16), wo_ref[...], preferred_element_type=F32)
    y_ref[0] = x + mod[2:3] * out


def _out_stage(x, mod3, norm_g, oa, ob, w_gates, w_pa, w_pb, w_out, subln_g):
    nb, s, d = x.shape
    const = pl.Buffered(1)
    n_gate = w_gates.shape[1]
    return pl.pallas_call(
        _out_kernel,
        grid=(nb, s // TS_OUT),
        in_specs=[
            pl.BlockSpec((1, TS_OUT, d), lambda b, t: (b, t, 0)),
            pl.BlockSpec((1, 3, d), lambda b, t: (b, 0, 0)),
            pl.BlockSpec((1, d), lambda b, t: (0, 0), pipeline_mode=const),
            pl.BlockSpec((1, TS_OUT, A_Q), lambda b, t: (b, t, 0)),
            pl.BlockSpec((1, TS_OUT, B_V), lambda b, t: (b, t, 0)),
            pl.BlockSpec((d, n_gate), lambda b, t: (0, 0), pipeline_mode=const),
            pl.BlockSpec((A_Q, d), lambda b, t: (0, 0), pipeline_mode=const),
            pl.BlockSpec((B_V, d), lambda b, t: (0, 0), pipeline_mode=const),
            pl.BlockSpec((d, d), lambda b, t: (0, 0), pipeline_mode=const),
            pl.BlockSpec((1, 2 * HEAD_DIM), lambda b, t: (0, 0), pipeline_mode=const),
        ],
        out_specs=pl.BlockSpec((1, TS_OUT, d), lambda b, t: (b, t, 0)),
        out_shape=jax.ShapeDtypeStruct((nb, s, d), F32),
        compiler_params=pltpu.CompilerParams(
            dimension_semantics=("arbitrary", "arbitrary"), vmem_limit_bytes=VMEM_LIMIT),
        name="out_stage",
    )(x, mod3, norm_g, oa, ob, w_gates, w_pa, w_pb, w_out, subln_g)


def _rope_coeff_tables(qn_a, kn_a, qn_b, kn_b, s):
    t = jnp.arange(s)
    row = (t // GRID_W).astype(F32)
    col = (t % GRID_W).astype(F32)
    inv_ax = AX_THETA ** (-jnp.arange(0, AX_DIM, 2, dtype=F32) / AX_DIM)
    ang_r = inv_ax[:, None] * row[None, :]
    ang_c = inv_ax[:, None] * col[None, :]
    ang_a = jnp.concatenate([ang_r, ang_r, ang_c, ang_c], axis=0)
    cos_a, sin_a = jnp.cos(ang_a), jnp.sin(ang_a)
    half = AX_DIM // 2
    sign_a = jnp.concatenate([-jnp.ones(half), jnp.ones(half), -jnp.ones(half), jnp.ones(half)]).astype(F32)
    perm_a = jnp.concatenate([jnp.arange(a, b) for a, b in _PERM_A])

    inv_p = ROPE_THETA ** (-jnp.arange(0, ROT_DIM, 2, dtype=F32) / ROT_DIM)
    ang_p = inv_p[:, None] * t.astype(F32)[None, :]
    ang_p = jnp.concatenate([ang_p, ang_p], axis=0)
    rest = HEAD_DIM - ROT_DIM
    cos_b = jnp.concatenate([jnp.cos(ang_p), jnp.ones((rest, s), F32)], axis=0)
    sin_b = jnp.concatenate([jnp.sin(ang_p), jnp.zeros((rest, s), F32)], axis=0)
    hb = ROT_DIM // 2
    sign_b = jnp.concatenate([-jnp.ones(hb), jnp.ones(hb), jnp.zeros(rest)]).astype(F32)
    perm_b = jnp.concatenate([jnp.arange(a, b) for a, b in _PERM_B])

    def pair(g, cos, sin, sign, perm, scale):
        g = g.astype(F32)
        return [scale * g[:, None] * cos, scale * (sign * g[perm])[:, None] * sin]

    tabs = (pair(qn_a, cos_a, sin_a, sign_a, perm_a, Q_SCALE) + pair(kn_a, cos_a, sin_a, sign_a, perm_a, 1.0)
            + pair(qn_b, cos_b, sin_b, sign_b, perm_b, Q_SCALE) + pair(kn_b, cos_b, sin_b, sign_b, perm_b, 1.0))
    return jnp.stack(tabs, axis=0)


def _layer(x, c, wts, tables, safe):
    (w_ada_bf, b_ada, norm_g, wT_att, w_gates, w_pa, w_pb, w_out, subln_g, lam_vecs) = wts
    nb = x.shape[0]
    mod3 = _ada_mod(c, w_ada_bf, b_ada).reshape(nb, 3, D_MODEL)
    qaT, vaT, qbT, vbT, ka, kb = _inproj(x, mod3, norm_g, wT_att, tables)

    def attend(online):
        def run(ops):
            qaT_, ka_, vaT_, qbT_, kb_, vbT_ = ops
            return (_attn_a(qaT_, ka_, vaT_, online=online),
                    _attn_b(lam_vecs, qbT_, kb_, vbT_, online=online))
        return run

    oa, ob = lax.cond(safe, attend(False), attend(True), (qaT, ka, vaT, qbT, kb, vbT))
    return _out_stage(x, mod3, norm_g, oa, ob, w_gates, w_pa, w_pb, w_out, subln_g)


def kernel(x_prompt, x_sample, c_prompt, c_sample, w_ada, b_ada, norm_g, w_in, qn_a, kn_a, qn_b, kn_b,
           lam_q1, lam_k1, lam_q2, lam_k2, subln_g, w_proj_a, w_proj_b, w_out):
    w = w_in[0]
    cols = lambda a, b: w[:, a:b]
    wT_att = jnp.concatenate(
        [cols(_O_QA, _O_KA), cols(_O_QB, _O_KB), cols(_O_KB, _O_VB), cols(_O_KA, _O_VA),
         cols(_O_VB, _O_ZB), cols(_O_VA, _O_ZA)], axis=1).T.astype(BF16)
    w_gates = jnp.concatenate(
        [cols(_O_ZA, _O_QB), cols(_O_ZB, _O_GA), cols(_O_GA, _O_END)], axis=1).astype(BF16)
    wts = (w_ada[0].astype(BF16), b_ada[0][None, :], norm_g[0][None, :], wT_att, w_gates,
           w_proj_a[0].astype(BF16), w_proj_b[0].astype(BF16), w_out[0].astype(BF16),
           subln_g[0][None, :], jnp.stack([lam_q1[0], lam_k1[0], lam_q2[0], lam_k2[0]], axis=0).astype(F32))
    tables = _rope_coeff_tables(qn_a[0], kn_a[0], qn_b[0], kn_b[0], SEQ)
    bound_a = math.sqrt(HEAD_DIM) * jnp.max(jnp.abs(qn_a[0])) * jnp.max(jnp.abs(kn_a[0]))
    bound_b = math.sqrt(HEAD_DIM) * jnp.max(jnp.abs(qn_b[0])) * jnp.max(jnp.abs(kn_b[0]))
    safe = jnp.maximum(bound_a, bound_b) <= SAFE_SCORE_BOUND
    y_prompt = _layer(x_prompt, c_prompt, wts, tables, safe)
    y_sample = _layer(x_sample, c_sample, wts, tables, safe)
    return (y_prompt, y_sample)
```
